```python
import jax, jax.numpy as jnp
from jax import lax
import numpy as np

D_MODEL = 1024
BATCH = 16
SEQ = 4096
DEPTH = 2

CHUNK = 64
ML_HEADS = 4
ML_HEAD_DIM = D_MODEL // 8
ML_WIDTH = ML_HEADS * ML_HEAD_DIM
CONV_WIDTH = 4
SB_HEADS = 8
SB_HEAD_DIM = D_MODEL // 16
SB_WIDTH = SB_HEADS * SB_HEAD_DIM
SB_BLOCK = 128
N_BRANCH = 2
D_FF = -(-8 * D_MODEL // (3 * 256)) * 256
EPS = 1e-6
IN_COLS = 2 * ML_WIDTH + ML_WIDTH + ML_WIDTH + 2 * ML_HEADS + 3 * SB_WIDTH + N_BRANCH * D_MODEL

kernel_name = "hybrid_mlstm_stickbreaking_gated_block"


def rmsnorm(x, g):
    xf = x.astype(jnp.float32)
    y = xf * lax.rsqrt(jnp.mean(xf * xf, axis=-1, keepdims=True) + EPS)
    return (y * g.astype(jnp.float32)).astype(x.dtype)


def causal_dwconv(x, w, b):
    C = x.shape[-1]
    y = lax.conv_general_dilated(
        x, w[:, None, :], window_strides=(1,), padding=[(CONV_WIDTH - 1, 0)],
        dimension_numbers=("NWC", "WIO", "NWC"), feature_group_count=C)
    return y + b


def mlstm_chunkwise(q, k, v, ig, fg):
    B, S, H, Dh = q.shape
    nc = S // CHUNK
    k = k * (Dh ** -0.5)
    lf = jax.nn.log_sigmoid(fg)

    def chunks4(a):
        return a.reshape(B, nc, CHUNK, H, Dh).transpose(1, 0, 3, 2, 4)

    def chunks3(a):
        return a.reshape(B, nc, CHUNK, H).transpose(1, 0, 3, 2)

    causal = jnp.tril(jnp.ones((CHUNK, CHUNK), dtype=bool))

    def step(carry, xs):
        C, n, m = carry
        qc, kc, vc, ic, fc = xs
        b = jnp.cumsum(fc, axis=-1)
        log_d = b[..., :, None] - b[..., None, :] + ic[..., None, :]
        log_d = jnp.where(causal, log_d, -jnp.inf)
        log_inter = b + m[..., None]
        m_t = jnp.maximum(log_inter, jnp.max(log_d, axis=-1))
        w_intra = jnp.exp(log_d - m_t[..., None])
        w_inter = jnp.exp(log_inter - m_t)
        s = jnp.einsum("bhld,bhmd->bhlm", qc, kc) * w_intra
        num = (w_inter[..., None] * jnp.einsum("bhld,bhde->bhle", qc, C)
               + jnp.einsum("bhlm,bhme->bhle", s, vc))
        den = w_inter * jnp.einsum("bhld,bhd->bhl", qc, n) + jnp.sum(s, axis=-1)
        h = num / jnp.maximum(jnp.abs(den), jnp.exp(-m_t))[..., None]
        b_last = b[..., -1]
        log_w = b_last[..., None] - b + ic
        m_new = jnp.maximum(b_last + m, jnp.max(log_w, axis=-1))
        w = jnp.exp(log_w - m_new[..., None])
        decay = jnp.exp(b_last + m - m_new)
        C_new = decay[..., None, None] * C + jnp.einsum("bhld,bhle->bhde", kc * w[..., None], vc)
        n_new = decay[..., None] * n + jnp.einsum("bhl,bhld->bhd", w, kc)
        return (C_new, n_new, m_new), h

    init = (jnp.zeros((B, H, Dh, Dh), jnp.float32),
            jnp.zeros((B, H, Dh), jnp.float32),
            jnp.zeros((B, H), jnp.float32))
    xs = (chunks4(q), chunks4(k), chunks4(v), chunks3(ig), chunks3(lf))
    _, h = lax.scan(step, init, xs)
    return h.transpose(1, 0, 3, 2, 4).reshape(B, S, H * Dh)


def stick_breaking(q, k, v):
    B, H, S, Dh = q.shape
    scale = Dh ** -0.5
    outs = []
    for start in range(0, S, SB_BLOCK):
        end = start + SB_BLOCK
        qb = q[:, :, start:end]
        kb = k[:, :, :end]
        vb = v[:, :, :end]
        z = jnp.einsum("bhtd,bhsd->bhts", qb, kb) * scale
        t_idx = start + jnp.arange(SB_BLOCK)
        s_idx = jnp.arange(end)
        mask = s_idx[None, :] < t_idx[:, None]
        u = jnp.where(mask, jax.nn.log_sigmoid(-z), 0.0)
        tail = lax.cumsum(u, axis=3, reverse=True) - u
        a = jnp.where(mask, jnp.exp(jax.nn.log_sigmoid(z) + tail), 0.0)
        outs.append(jnp.einsum("bhts,bhsd->bhtd", a, vb))
    o = jnp.concatenate(outs, axis=2)
    return o.transpose(0, 2, 1, 3).reshape(B, S, H * Dh)


def head_rms(x, g):
    xf = x.astype(jnp.float32)
    return xf * lax.rsqrt(jnp.mean(xf * xf, axis=-1, keepdims=True) + EPS) * g.astype(jnp.float32)


def setup_inputs(seed: int = 0) -> dict:
    key = jax.random.key(seed)
    ks = jax.random.split(key, 16)
    nrm = jax.random.normal
    f32 = jnp.float32
    x = nrm(ks[0], (BATCH, SEQ, D_MODEL), f32)
    g_mix = 1.0 + 0.02 * nrm(ks[1], (DEPTH, D_MODEL), f32)
    w_in = nrm(ks[2], (DEPTH, D_MODEL, IN_COLS), f32) * D_MODEL ** -0.5
    conv_w = nrm(ks[3], (DEPTH, CONV_WIDTH, 2 * ML_WIDTH), f32) * CONV_WIDTH ** -0.5
    conv_b = 0.02 * nrm(ks[4], (DEPTH, 2 * ML_WIDTH), f32)
    i_bias = 0.1 * nrm(ks[5], (DEPTH, ML_HEADS), f32)
    f_bias = jnp.linspace(3.0, 6.0, ML_HEADS, dtype=f32)[None, :] + 0.1 * nrm(ks[6], (DEPTH, ML_HEADS), f32)
    b_gates = jnp.concatenate([i_bias, f_bias], axis=-1)
    g_q = 1.0 + 0.02 * nrm(ks[7], (DEPTH, SB_HEAD_DIM), f32)
    g_k = 1.0 + 0.02 * nrm(ks[8], (DEPTH, SB_HEAD_DIM), f32)
    w_br_a = nrm(ks[9], (DEPTH, ML_WIDTH, D_MODEL), f32) * ML_WIDTH ** -0.5
    w_br_b = nrm(ks[10], (DEPTH, SB_WIDTH, D_MODEL), f32) * SB_WIDTH ** -0.5
    w_out = nrm(ks[11], (DEPTH, D_MODEL, D_MODEL), f32) * D_MODEL ** -0.5
    g_ffn = 1.0 + 0.02 * nrm(ks[12], (DEPTH, D_MODEL), f32)
    w_gu = nrm(ks[13], (DEPTH, D_MODEL, 2 * D_FF), f32) * D_MODEL ** -0.5
    w_down = nrm(ks[14], (DEPTH, D_FF, D_MODEL), f32) * D_FF ** -0.5
    return {"x": x, "g_mix": g_mix, "w_in": w_in, "conv_w": conv_w, "conv_b": conv_b,
            "b_gates": b_gates, "g_q": g_q, "g_k": g_k, "w_br_a": w_br_a, "w_br_b": w_br_b,
            "w_out": w_out, "g_ffn": g_ffn, "w_gu": w_gu, "w_down": w_down}


def reference(x, g_mix, w_in, conv_w, conv_b, b_gates, g_q, g_k, w_br_a, w_br_b,
              w_out, g_ffn, w_gu, w_down):
    B, S, _ = x.shape
    sizes = [2 * ML_WIDTH, ML_WIDTH, ML_WIDTH, 2 * ML_HEADS, SB_WIDTH, SB_WIDTH, SB_WIDTH]
    cuts = list(np.cumsum(sizes))
    for l in range(DEPTH):
        h = rmsnorm(x, g_mix[l])
        proj = h @ w_in[l]
        qk_pre, v_m, o_m, gates, q_s, k_s, v_s, gate_pre = jnp.split(proj, cuts, axis=-1)

        qk_m = jax.nn.silu(causal_dwconv(qk_pre, conv_w[l], conv_b[l]))
        q_m, k_m = jnp.split(qk_m.astype(jnp.float32), 2, axis=-1)
        gates = gates.astype(jnp.float32) + b_gates[l].astype(jnp.float32)
        ig, fg = gates[..., :ML_HEADS], gates[..., ML_HEADS:]
        hd = (B, S, ML_HEADS, ML_HEAD_DIM)
        h_m = mlstm_chunkwise(q_m.reshape(hd), k_m.reshape(hd),
                              v_m.astype(jnp.float32).reshape(hd), ig, fg)
        y_a = (jax.nn.sigmoid(o_m.astype(jnp.float32)) * h_m).astype(x.dtype)

        sd = (B, S, SB_HEADS, SB_HEAD_DIM)
        qs = head_rms(q_s.reshape(sd), g_q[l]).transpose(0, 2, 1, 3)
        kss = head_rms(k_s.reshape(sd), g_k[l]).transpose(0, 2, 1, 3)
        vs = v_s.astype(jnp.float32).reshape(sd).transpose(0, 2, 1, 3)
        y_b = stick_breaking(qs, kss, vs).astype(x.dtype)

        g = jax.nn.sigmoid(gate_pre).reshape(B, S, N_BRANCH, D_MODEL)
        mix = g[..., 0, :] * (y_a @ w_br_a[l]) + g[..., 1, :] * (y_b @ w_br_b[l])
        x = x + mix @ w_out[l]

        h2 = rmsnorm(x, g_ffn[l])
        gt, up = jnp.split(h2 @ w_gu[l], 2, axis=-1)
        x = x + (jax.nn.silu(gt) * up) @ w_down[l]
    return x
```

```python
import functools

import jax
import jax.numpy as jnp
import numpy as np
from jax import lax
from jax.experimental import pallas as pl
from jax.experimental.pallas import tpu as pltpu

D_MODEL = 1024
ML_HEADS = 4
ML_HEAD_DIM = 128
ML_WIDTH = ML_HEADS * ML_HEAD_DIM
CONV_WIDTH = 4
SB_HEADS = 8
SB_HEAD_DIM = 64
SB_WIDTH = SB_HEADS * SB_HEAD_DIM
N_BRANCH = 2
D_FF = 2816
EPS = 1e-6

LANES = 128
SUBLANES = 8
GATE_PAD = LANES
ROW_TILE = 512
ML_CHUNK = 256
SB_TILE = 256
FF_CHUNK = 256
VMEM_LIMIT = 56 * 1024 * 1024

F32 = jnp.float32
BF16 = jnp.bfloat16

C_QK = 0
C_VM = C_QK + 2 * ML_WIDTH
C_OM = C_VM + ML_WIDTH
C_QS = C_OM + ML_WIDTH
C_KS = C_QS + SB_WIDTH
C_VS = C_KS + SB_WIDTH
C_GP = C_VS + SB_WIDTH
C_GATES = C_GP + N_BRANCH * D_MODEL
C_END = C_GATES + GATE_PAD


def _const_spec(shape):
    nd = len(shape)
    return pl.BlockSpec(shape, lambda *_: (0,) * nd, pipeline_mode=pl.Buffered(1))


def _split2(a):
    hi = a.astype(BF16)
    lo = (a - hi.astype(F32)).astype(BF16)
    return hi, lo


def _split3(a):
    hi = a.astype(BF16)
    r = a - hi.astype(F32)
    mid = r.astype(BF16)
    lo = (r - mid.astype(F32)).astype(BF16)
    return hi, mid, lo


def _dot(a, b):
    return jnp.dot(a, b, preferred_element_type=F32)


def _dot_nt(a, b):
    return lax.dot_general(a, b, (((1,), (1,)), ((), ())), preferred_element_type=F32)


def _sigmoid(x):
    return 1.0 / (1.0 + jnp.exp(-x))


def _inproj_kernel(x_ref, gmix_ref, w_ref, bg_ref, gq_ref, gk_ref, bd_ref,
                   qk_ref, vm_ref, om_ref, qs_ref, ks_ref, vs_ref, gp_ref, gates_ref):
    x = x_ref[...]
    ms = jnp.mean(x * x, axis=-1, keepdims=True)
    h = (x * lax.rsqrt(ms + EPS) * gmix_ref[...]).astype(BF16)

    def proj(a, b):
        return _dot(h, w_ref[:, a:b])

    def head_rms(y, g):
        hi, lo = _split2(y * y)
        gs = _dot(hi, bd_ref[...]) + _dot(lo, bd_ref[...])
        return y * lax.rsqrt(gs * (1.0 / SB_HEAD_DIM) + EPS) * g

    qk_ref[...] = proj(C_QK, C_VM)
    vm_ref[...] = proj(C_VM, C_OM).astype(BF16)
    om_ref[...] = _sigmoid(proj(C_OM, C_QS)).astype(BF16)
    qs_ref[...] = (head_rms(proj(C_QS, C_KS), gq_ref[...]) * (SB_HEAD_DIM ** -0.5)).astype(BF16)
    ks_ref[...] = head_rms(proj(C_KS, C_VS), gk_ref[...]).astype(BF16)
    vs_ref[...] = proj(C_VS, C_GP).astype(BF16)
    gp_ref[...] = _sigmoid(proj(C_GP, C_GATES)).astype(BF16)
    gates_ref[...] = proj(C_GATES, C_END) + bg_ref[...]


def _inproj(x2, gmix, w_perm, bg, gq, gk, bd):
    T = x2.shape[0]
    tm = ROW_TILE
    row = lambda n: pl.BlockSpec((tm, n), lambda i: (i, 0))
    out_shapes = (
        jax.ShapeDtypeStruct((T, 2 * ML_WIDTH), F32),
        jax.ShapeDtypeStruct((T, ML_WIDTH), BF16),
        jax.ShapeDtypeStruct((T, ML_WIDTH), BF16),
        jax.ShapeDtypeStruct((T, SB_WIDTH), BF16),
        jax.ShapeDtypeStruct((T, SB_WIDTH), BF16),
        jax.ShapeDtypeStruct((T, SB_WIDTH), BF16),
        jax.ShapeDtypeStruct((T, N_BRANCH * D_MODEL), BF16),
        jax.ShapeDtypeStruct((T, GATE_PAD), F32),
    )
    return pl.pallas_call(
        _inproj_kernel,
        grid=(T // tm,),
        in_specs=[row(D_MODEL), _const_spec((1, D_MODEL)), _const_spec((D_MODEL, C_END)),
                  _const_spec((1, GATE_PAD)), _const_spec((1, SB_WIDTH)), _const_spec((1, SB_WIDTH)),
                  _const_spec((SB_WIDTH, SB_WIDTH))],
        out_specs=(row(2 * ML_WIDTH), row(ML_WIDTH), row(ML_WIDTH), row(SB_WIDTH), row(SB_WIDTH),
                   row(SB_WIDTH), row(N_BRANCH * D_MODEL), row(GATE_PAD)),
        out_shape=out_shapes,
        compiler_params=pltpu.CompilerParams(dimension_semantics=("parallel",),
                                             vmem_limit_bytes=VMEM_LIMIT),
        name="inproj",
    )(x2, gmix, w_perm, bg, gq, gk, bd)


def _log_sigmoid(x):
    return jnp.minimum(x, 0.0) - jnp.log1p(jnp.exp(-jnp.abs(x)))


def _mlstm_kernel(qk_ref, vm_ref, om_ref, gcol_ref, grow_ref, cw_ref, cb_ref, tril_ref, triu_ref,
                  out_ref, xbuf, c_ref, m_ref):
    L = ML_CHUNK
    H = ML_HEADS
    Dh = ML_HEAD_DIM
    halo = SUBLANES

    @pl.when(pl.program_id(1) == 0)
    def _():
        xbuf[0:halo, :] = jnp.zeros((halo, 2 * ML_WIDTH), F32)
        c_ref[...] = jnp.zeros_like(c_ref)
        m_ref[...] = jnp.zeros_like(m_ref)

    xbuf[halo:halo + L, :] = qk_ref[...]
    acc = jnp.broadcast_to(cb_ref[...], (L, 2 * ML_WIDTH))
    for j in range(CONV_WIDTH):
        off = halo - (CONV_WIDTH - 1) + j
        acc = acc + cw_ref[j:j + 1, :] * xbuf[off:off + L, :]
    xbuf[0:halo, :] = xbuf[L:L + halo, :]
    qkc = acc * _sigmoid(acc)
    q_all = qkc[:, :ML_WIDTH]
    k_all = qkc[:, ML_WIDTH:] * (Dh ** -0.5)

    gc = gcol_ref[...]
    gr = grow_ref[...]
    lfc = _log_sigmoid(gc)
    lfr = _log_sigmoid(gr)
    bcol = sum(_dot(tril_ref[...], p) for p in _split3(lfc))
    brow = sum(_dot(p, triu_ref[...]) for p in _split3(lfr))

    row_i = lax.broadcasted_iota(jnp.int32, (L, L), 0)
    col_i = lax.broadcasted_iota(jnp.int32, (L, L), 1)
    causal = col_i <= row_i
    ones_aug = jnp.ones((L, Dh), BF16)

    for h in range(H):
        bc = bcol[:, H + h:H + h + 1]
        br = brow[H + h:H + h + 1, :]
        ir = gr[h:h + 1, :]
        ic = gc[:, h:h + 1]
        mprev = m_ref[h:h + 1, 0:1]

        logd = jnp.where(causal, bc - br + ir, -jnp.inf)
        mt = jnp.maximum(bc + mprev, jnp.max(logd, axis=-1, keepdims=True))
        w_intra = jnp.exp(logd - mt)
        w_inter = jnp.exp(bc + mprev - mt)

        sl = slice(h * Dh, (h + 1) * Dh)
        qh = q_all[:, sl].astype(BF16)
        kf = k_all[:, sl]
        kh = kf.astype(BF16)
        vaug = jnp.concatenate([vm_ref[:, sl], ones_aug], axis=1)
        caug = c_ref[h]

        s = _dot_nt(qh, kh) * w_intra
        u = w_inter * _dot(qh, caug.astype(BF16)) + _dot(s.astype(BF16), vaug)
        num = u[:, :Dh]
        den = u[:, Dh:]
        hout = num / jnp.maximum(jnp.abs(den), jnp.exp(-mt))
        out_ref[:, sl] = (om_ref[:, sl].astype(F32) * hout).astype(BF16)

        blast = bc[L - 1:L, :]
        logw_c = blast - bc + ic
        logw_r = blast - br + ir
        mnew = jnp.maximum(blast + mprev, jnp.max(logw_r, axis=-1, keepdims=True))
        w_c = jnp.exp(logw_c - mnew)
        decay = jnp.exp(blast + mprev - mnew)
        kwt = (kf * w_c).T.astype(BF16)
        c_ref[h] = decay * caug + _dot(kwt, vaug)
        m_ref[h:h + 1, :] = jnp.broadcast_to(mnew, (1, LANES))


def _mlstm(qk_pre, vm, om, gcol, grow, conv_w, conv_b, tril, triu):
    B, S, _ = qk_pre.shape
    L = ML_CHUNK
    blk = lambda n: pl.BlockSpec((None, L, n), lambda b, c: (b, c, 0))
    return pl.pallas_call(
        _mlstm_kernel,
        grid=(B, S // L),
        in_specs=[blk(2 * ML_WIDTH), blk(ML_WIDTH), blk(ML_WIDTH), blk(2 * ML_HEADS),
                  pl.BlockSpec((None, 2 * SUBLANES, L), lambda b, c: (b, 0, c)),
                  _const_spec((CONV_WIDTH, 2 * ML_WIDTH)), _const_spec((1, 2 * ML_WIDTH)),
                  _const_spec((L, L)), _const_spec((L, L))],
        out_specs=blk(ML_WIDTH),
        out_shape=jax.ShapeDtypeStruct((B, S, ML_WIDTH), BF16),
        scratch_shapes=[pltpu.VMEM((L + SUBLANES, 2 * ML_WIDTH), F32),
                        pltpu.VMEM((ML_HEADS, ML_HEAD_DIM, 2 * ML_HEAD_DIM), F32),
                        pltpu.VMEM((SUBLANES, LANES), F32)],
        compiler_params=pltpu.CompilerParams(dimension_semantics=("parallel", "arbitrary"),
                                             vmem_limit_bytes=VMEM_LIMIT),
        name="mlstm",
    )(qk_pre, vm, om, gcol, grow, conv_w, conv_b, tril, triu)


def _sb_kernel(q_ref, k_ref, v_ref, tt_ref, out_ref):
    T = SB_TILE
    nsub = T // LANES
    qi = pl.program_id(2)
    qp = q_ref[...]
    lane = lax.broadcasted_iota(jnp.int32, (1, LANES), 1)
    first = lane < SB_HEAD_DIM
    zero = jnp.zeros_like(qp)
    qm = (jnp.where(first, qp, zero), jnp.where(first, zero, qp))

    row_i = lax.broadcasted_iota(jnp.int32, (T, T), 0)
    col_i = lax.broadcasted_iota(jnp.int32, (T, T), 1)
    strict = col_i < row_i

    def block(j, state, masked):
        start = pl.multiple_of(j * T, T)
        ks = k_ref[pl.ds(start, T), :]
        vs = v_ref[pl.ds(start, T), :]
        new = []
        for hh in range(2):
            carry, acc = state[2 * hh], state[2 * hh + 1]
            z = _dot_nt(qm[hh], ks)
            ls = jnp.minimum(z, 0.0) - jnp.log(1.0 + jnp.exp(-jnp.abs(z)))
            u = ls - z
            if masked:
                u = jnp.where(strict, u, 0.0)
            parts = [None] * nsub
            for c in reversed(range(nsub)):
                cs = slice(c * LANES, (c + 1) * LANES)
                hi, lo = _split2(u[:, cs])
                res = _dot(jnp.concatenate([hi, lo], axis=1), tt_ref[...])
                tail = res[:, :LANES] + carry
                carry = carry + res[:, LANES:]
                a = jnp.exp(ls[:, cs] + tail)
                if masked:
                    a = jnp.where(strict[:, cs], a, 0.0)
                parts[c] = a.astype(BF16)
            acc = acc + _dot(jnp.concatenate(parts, axis=1), vs)
            new += [carry, acc]
        return tuple(new)

    zeros = jnp.zeros((T, LANES), F32)
    state = block(qi, (zeros, zeros, zeros, zeros), True)
    state = lax.fori_loop(0, qi, lambda i, st: block(qi - 1 - i, st, False), state)
    out_ref[...] = jnp.where(first, state[1], state[3]).astype(BF16)


def _stick_breaking(qs, ks, vs, tt):
    B, S, _ = qs.shape
    T = SB_TILE
    return pl.pallas_call(
        _sb_kernel,
        grid=(B, SB_WIDTH // LANES, S // T),
        in_specs=[pl.BlockSpec((None, T, LANES), lambda b, p, i: (b, i, p)),
                  pl.BlockSpec((None, S, LANES), lambda b, p, i: (b, 0, p)),
                  pl.BlockSpec((None, S, LANES), lambda b, p, i: (b, 0, p)),
                  _const_spec((2 * LANES, 2 * LANES))],
        out_specs=pl.BlockSpec((None, T, LANES), lambda b, p, i: (b, i, p)),
        out_shape=jax.ShapeDtypeStruct((B, S, SB_WIDTH), BF16),
        compiler_params=pltpu.CompilerParams(
            dimension_semantics=("parallel", "parallel", "arbitrary"),
            vmem_limit_bytes=VMEM_LIMIT),
        name="stickbreak",
    )(qs, ks, vs, tt)


def _merge_kernel(x_ref, ya_ref, yb_ref, g_ref, wa_ref, wb_ref, wo_ref, gffn_ref, xo_ref, h2_ref):
    pa = _dot(ya_ref[...], wa_ref[...])
    pb = _dot(yb_ref[...], wb_ref[...])
    mix = g_ref[:, :D_MODEL].astype(F32) * pa + g_ref[:, D_MODEL:].astype(F32) * pb
    xn = x_ref[...] + _dot(mix.astype(BF16), wo_ref[...])
    xo_ref[...] = xn
    ms = jnp.mean(xn * xn, axis=-1, keepdims=True)
    h2_ref[...] = (xn * lax.rsqrt(ms + EPS) * gffn_ref[...]).astype(BF16)


def _merge(x2, ya, yb, g, wa, wb, wo, gffn):
    T = x2.shape[0]
    tm = ROW_TILE
    row = lambda n: pl.BlockSpec((tm, n), lambda i: (i, 0))
    return pl.pallas_call(
        _merge_kernel,
        grid=(T // tm,),
        in_specs=[row(D_MODEL), row(ML_WIDTH), row(SB_WIDTH), row(N_BRANCH * D_MODEL),
                  _const_spec((ML_WIDTH, D_MODEL)), _const_spec((SB_WIDTH, D_MODEL)),
                  _const_spec((D_MODEL, D_MODEL)), _const_spec((1, D_MODEL))],
        out_specs=(row(D_MODEL), row(D_MODEL)),
        out_shape=(jax.ShapeDtypeStruct((T, D_MODEL), F32), jax.ShapeDtypeStruct((T, D_MODEL), BF16)),
        compiler_params=pltpu.CompilerParams(dimension_semantics=("parallel",),
                                             vmem_limit_bytes=VMEM_LIMIT),
        name="merge",
    )(x2, ya, yb, g, wa, wb, wo, gffn)


def _ffn_kernel(x_ref, h2_ref, wgu_ref, wd_ref, out_ref):
    h2 = h2_ref[...]
    acc = x_ref[...]
    for c in range(D_FF // FF_CHUNK):
        lo = c * FF_CHUNK
        gt = _dot(h2, wgu_ref[:, lo:lo + FF_CHUNK])
        up = _dot(h2, wgu_ref[:, D_FF + lo:D_FF + lo + FF_CHUNK])
        act = (gt * _sigmoid(gt) * up).astype(BF16)
        acc = acc + _dot(act, wd_ref[lo:lo + FF_CHUNK, :])
    out_ref[...] = acc


def _ffn(x2, h2, wgu, wd):
    T = x2.shape[0]
    tm = ROW_TILE
    row = lambda n: pl.BlockSpec((tm, n), lambda i: (i, 0))
    return pl.pallas_call(
        _ffn_kernel,
        grid=(T // tm,),
        in_specs=[row(D_MODEL), row(D_MODEL), _const_spec((D_MODEL, 2 * D_FF)),
                  _const_spec((D_FF, D_MODEL))],
        out_specs=row(D_MODEL),
        out_shape=jax.ShapeDtypeStruct((T, D_MODEL), F32),
        compiler_params=pltpu.CompilerParams(dimension_semantics=("parallel",),
                                             vmem_limit_bytes=VMEM_LIMIT),
        name="ffn",
    )(x2, h2, wgu, wd)


def _constants():
    L = ML_CHUNK
    r = np.arange(L)
    tril = (r[None, :] <= r[:, None]).astype(np.float32)
    g = np.arange(SB_WIDTH) // SB_HEAD_DIM
    bd = (g[:, None] == g[None, :]).astype(np.float32)
    j = np.arange(2 * LANES) % LANES
    s = np.arange(2 * LANES)
    tt = np.where(s[None, :] < LANES, j[:, None] > s[None, :], True).astype(np.float32)
    return (jnp.asarray(tril, BF16), jnp.asarray(tril.T, BF16), jnp.asarray(bd, BF16),
            jnp.asarray(tt, BF16))


def kernel(x, g_mix, w_in, conv_w, conv_b, b_gates, g_q, g_k, w_br_a, w_br_b, w_out, g_ffn, w_gu,
           w_down):
    B, S, D = x.shape
    depth = g_mix.shape[0]
    T = B * S
    tril, triu, bd, tt = _constants()
    n_gate = 2 * ML_HEADS
    o_gates = 2 * ML_WIDTH + 2 * ML_WIDTH

    x2 = x.reshape(T, D)
    for l in range(depth):
        w = w_in[l]
        w_perm = jnp.concatenate(
            [w[:, :o_gates], w[:, o_gates + n_gate:],
             jnp.pad(w[:, o_gates:o_gates + n_gate], ((0, 0), (0, GATE_PAD - n_gate)))],
            axis=1).astype(BF16)
        bg = jnp.pad(b_gates[l], (0, GATE_PAD - n_gate)).reshape(1, GATE_PAD)
        gq = jnp.tile(g_q[l], SB_HEADS).reshape(1, SB_WIDTH)
        gk = jnp.tile(g_k[l], SB_HEADS).reshape(1, SB_WIDTH)

        qk_pre, vm, om, qs, ks, vs, gp, gates = _inproj(
            x2, g_mix[l].reshape(1, D), w_perm, bg, gq, gk, bd)

        gcol = gates[:, :n_gate].reshape(B, S, n_gate)
        grow = jnp.pad(jnp.swapaxes(gcol, 1, 2), ((0, 0), (0, 2 * SUBLANES - n_gate), (0, 0)))
        ya = _mlstm(qk_pre.reshape(B, S, -1), vm.reshape(B, S, -1), om.reshape(B, S, -1),
                    gcol, grow, conv_w[l], conv_b[l].reshape(1, -1), tril, triu)
        yb = _stick_breaking(qs.reshape(B, S, -1), ks.reshape(B, S, -1), vs.reshape(B, S, -1), tt)

        x2, h2 = _merge(x2, ya.reshape(T, -1), yb.reshape(T, -1), gp,
                        w_br_a[l].astype(BF16), w_br_b[l].astype(BF16), w_out[l].astype(BF16),
                        g_ffn[l].reshape(1, D))
        x2 = _ffn(x2, h2, w_gu[l].astype(BF16), w_down[l].astype(BF16))
    return x2.reshape(B, S, D)
```

```python
import functools

import jax
import jax.numpy as jnp
import numpy as np
from jax import lax
from jax.experimental import pallas as pl
from jax.experimental.pallas import tpu as pltpu

D_MODEL = 1024
ML_HEADS = 4
ML_HEAD_DIM = 128
ML_WIDTH = ML_HEADS * ML_HEAD_DIM
CONV_WIDTH = 4
SB_HEADS = 8
SB_HEAD_DIM = 64
SB_WIDTH = SB_HEADS * SB_HEAD_DIM
N_BRANCH = 2
D_FF = 2816
EPS = 1e-6

LANES = 128
SUBLANES = 8
GATE_PAD = LANES
ROW_TILE = 512
ML_CHUNK = 256
SB_TILE = 512
SB_GROUP = 256
LOG2E = 1.4426950408889634
FF_CHUNK = 256
VMEM_LIMIT = 56 * 1024 * 1024

F32 = jnp.float32
BF16 = jnp.bfloat16

C_QK = 0
C_VM = C_QK + 2 * ML_WIDTH
C_OM = C_VM + ML_WIDTH
C_QS = C_OM + ML_WIDTH
C_KS = C_QS + SB_WIDTH
C_VS = C_KS + SB_WIDTH
C_GP = C_VS + SB_WIDTH
C_GATES = C_GP + N_BRANCH * D_MODEL
C_END = C_GATES + GATE_PAD


def _const_spec(shape):
    nd = len(shape)
    return pl.BlockSpec(shape, lambda *_: (0,) * nd, pipeline_mode=pl.Buffered(1))


def _split2(a):
    hi = a.astype(BF16)
    lo = (a - hi.astype(F32)).astype(BF16)
    return hi, lo


def _split3(a):
    hi = a.astype(BF16)
    r = a - hi.astype(F32)
    mid = r.astype(BF16)
    lo = (r - mid.astype(F32)).astype(BF16)
    return hi, mid, lo


def _dot(a, b):
    return jnp.dot(a, b, preferred_element_type=F32)


def _dot_nt(a, b):
    return lax.dot_general(a, b, (((1,), (1,)), ((), ())), preferred_element_type=F32)


def _sigmoid(x):
    return 1.0 / (1.0 + jnp.exp(-x))


def _inproj_kernel(x_ref, gmix_ref, w_ref, bg_ref, gq_ref, gk_ref, bd_ref,
                   qk_ref, vm_ref, om_ref, qs_ref, ks_ref, vs_ref, gp_ref, gates_ref):
    x = x_ref[...]
    ms = jnp.mean(x * x, axis=-1, keepdims=True)
    h = (x * lax.rsqrt(ms + EPS) * gmix_ref[...]).astype(BF16)

    def proj(a, b):
        return _dot(h, w_ref[:, a:b])

    def head_rms(y, g):
        hi, lo = _split2(y * y)
        gs = _dot(hi, bd_ref[...]) + _dot(lo, bd_ref[...])
        return y * lax.rsqrt(gs * (1.0 / SB_HEAD_DIM) + EPS) * g

    qk_ref[...] = proj(C_QK, C_VM)
    vm_ref[...] = proj(C_VM, C_OM).astype(BF16)
    om_ref[...] = _sigmoid(proj(C_OM, C_QS)).astype(BF16)
    qs_ref[...] = (head_rms(proj(C_QS, C_KS), gq_ref[...]) * (SB_HEAD_DIM ** -0.5 * LOG2E)).astype(BF16)
    ks_ref[...] = head_rms(proj(C_KS, C_VS), gk_ref[...]).astype(BF16)
    vs_ref[...] = proj(C_VS, C_GP).astype(BF16)
    gp_ref[...] = _sigmoid(proj(C_GP, C_GATES)).astype(BF16)
    gates_ref[...] = proj(C_GATES, C_END) + bg_ref[...]


def _inproj(x2, gmix, w_perm, bg, gq, gk, bd):
    T = x2.shape[0]
    tm = ROW_TILE
    row = lambda n: pl.BlockSpec((tm, n), lambda i: (i, 0))
    out_shapes = (
        jax.ShapeDtypeStruct((T, 2 * ML_WIDTH), F32),
        jax.ShapeDtypeStruct((T, ML_WIDTH), BF16),
        jax.ShapeDtypeStruct((T, ML_WIDTH), BF16),
        jax.ShapeDtypeStruct((T, SB_WIDTH), BF16),
        jax.ShapeDtypeStruct((T, SB_WIDTH), BF16),
        jax.ShapeDtypeStruct((T, SB_WIDTH), BF16),
        jax.ShapeDtypeStruct((T, N_BRANCH * D_MODEL), BF16),
        jax.ShapeDtypeStruct((T, GATE_PAD), F32),
    )
    return pl.pallas_call(
        _inproj_kernel,
        grid=(T // tm,),
        in_specs=[row(D_MODEL), _const_spec((1, D_MODEL)), _const_spec((D_MODEL, C_END)),
                  _const_spec((1, GATE_PAD)), _const_spec((1, SB_WIDTH)), _const_spec((1, SB_WIDTH)),
                  _const_spec((SB_WIDTH, SB_WIDTH))],
        out_specs=(row(2 * ML_WIDTH), row(ML_WIDTH), row(ML_WIDTH), row(SB_WIDTH), row(SB_WIDTH),
                   row(SB_WIDTH), row(N_BRANCH * D_MODEL), row(GATE_PAD)),
        out_shape=out_shapes,
        compiler_params=pltpu.CompilerParams(dimension_semantics=("parallel",),
                                             vmem_limit_bytes=VMEM_LIMIT),
        name="inproj",
    )(x2, gmix, w_perm, bg, gq, gk, bd)


def _log_sigmoid(x):
    return jnp.minimum(x, 0.0) - jnp.log1p(jnp.exp(-jnp.abs(x)))


def _mlstm_kernel(qk_ref, vm_ref, om_ref, gcol_ref, grow_ref, cw_ref, cb_ref, tril_ref, triu_ref,
                  out_ref, xbuf, c_ref, m_ref):
    L = ML_CHUNK
    H = ML_HEADS
    Dh = ML_HEAD_DIM
    halo = SUBLANES

    @pl.when(pl.program_id(1) == 0)
    def _():
        xbuf[0:halo, :] = jnp.zeros((halo, 2 * ML_WIDTH), F32)
        c_ref[...] = jnp.zeros_like(c_ref)
        m_ref[...] = jnp.zeros_like(m_ref)

    xbuf[halo:halo + L, :] = qk_ref[...]
    acc = jnp.broadcast_to(cb_ref[...], (L, 2 * ML_WIDTH))
    for j in range(CONV_WIDTH):
        off = halo - (CONV_WIDTH - 1) + j
        acc = acc + cw_ref[j:j + 1, :] * xbuf[off:off + L, :]
    xbuf[0:halo, :] = xbuf[L:L + halo, :]
    qkc = acc * _sigmoid(acc)
    q_all = qkc[:, :ML_WIDTH]
    k_all = qkc[:, ML_WIDTH:] * (Dh ** -0.5)

    gc = gcol_ref[...]
    gr = grow_ref[...]
    lfc = _log_sigmoid(gc)
    lfr = _log_sigmoid(gr)
    bcol = sum(_dot(tril_ref[...], p) for p in _split3(lfc))
    brow = sum(_dot(p, triu_ref[...]) for p in _split3(lfr))

    row_i = lax.broadcasted_iota(jnp.int32, (L, L), 0)
    col_i = lax.broadcasted_iota(jnp.int32, (L, L), 1)
    causal = col_i <= row_i
    ones_aug = jnp.ones((L, Dh), BF16)

    for h in range(H):
        bc = bcol[:, H + h:H + h + 1]
        br = brow[H + h:H + h + 1, :]
        ir = gr[h:h + 1, :]
        ic = gc[:, h:h + 1]
        mprev = m_ref[h:h + 1, 0:1]

        logd = jnp.where(causal, bc - br + ir, -jnp.inf)
        mt = jnp.maximum(bc + mprev, jnp.max(logd, axis=-1, keepdims=True))
        w_intra = jnp.exp(logd - mt)
        w_inter = jnp.exp(bc + mprev - mt)

        sl = slice(h * Dh, (h + 1) * Dh)
        qh = q_all[:, sl].astype(BF16)
        kf = k_all[:, sl]
        kh = kf.astype(BF16)
        vaug = jnp.concatenate([vm_ref[:, sl], ones_aug], axis=1)
        caug = c_ref[h]

        s = _dot_nt(qh, kh) * w_intra
        u = w_inter * _dot(qh, caug.astype(BF16)) + _dot(s.astype(BF16), vaug)
        num = u[:, :Dh]
        den = u[:, Dh:]
        hout = num / jnp.maximum(jnp.abs(den), jnp.exp(-mt))
        out_ref[:, sl] = (om_ref[:, sl].astype(F32) * hout).astype(BF16)

        blast = bc[L - 1:L, :]
        logw_c = blast - bc + ic
        logw_r = blast - br + ir
        mnew = jnp.maximum(blast + mprev, jnp.max(logw_r, axis=-1, keepdims=True))
        w_c = jnp.exp(logw_c - mnew)
        decay = jnp.exp(blast + mprev - mnew)
        kwt = (kf * w_c).T.astype(BF16)
        c_ref[h] = decay * caug + _dot(kwt, vaug)
        m_ref[h:h + 1, :] = jnp.broadcast_to(mnew, (1, LANES))


def _mlstm(qk_pre, vm, om, gcol, grow, conv_w, conv_b, tril, triu):
    B, S, _ = qk_pre.shape
    L = ML_CHUNK
    blk = lambda n: pl.BlockSpec((None, L, n), lambda b, c: (b, c, 0))
    return pl.pallas_call(
        _mlstm_kernel,
        grid=(B, S // L),
        in_specs=[blk(2 * ML_WIDTH), blk(ML_WIDTH), blk(ML_WIDTH), blk(2 * ML_HEADS),
                  pl.BlockSpec((None, 2 * SUBLANES, L), lambda b, c: (b, 0, c)),
                  _const_spec((CONV_WIDTH, 2 * ML_WIDTH)), _const_spec((1, 2 * ML_WIDTH)),
                  _const_spec((L, L)), _const_spec((L, L))],
        out_specs=blk(ML_WIDTH),
        out_shape=jax.ShapeDtypeStruct((B, S, ML_WIDTH), BF16),
        scratch_shapes=[pltpu.VMEM((L + SUBLANES, 2 * ML_WIDTH), F32),
                        pltpu.VMEM((ML_HEADS, ML_HEAD_DIM, 2 * ML_HEAD_DIM), F32),
                        pltpu.VMEM((SUBLANES, LANES), F32)],
        compiler_params=pltpu.CompilerParams(dimension_semantics=("parallel", "arbitrary"),
                                             vmem_limit_bytes=VMEM_LIMIT),
        name="mlstm",
    )(qk_pre, vm, om, gcol, grow, conv_w, conv_b, tril, triu)


def _neg_abs(x):
    bits = lax.bitcast_convert_type(x, jnp.uint32) | jnp.uint32(0x80000000)
    return lax.bitcast_convert_type(bits, F32)


def _sb_kernel(q_ref, k_ref, v_ref, tt_ref, out_ref, carry_ref, acc_ref):
    T = SB_TILE
    G = SB_GROUP
    qi = pl.program_id(2)
    qp = q_ref[...]
    lane = lax.broadcasted_iota(jnp.int32, (1, LANES), 1)
    first = lane < SB_HEAD_DIM
    zero = jnp.zeros_like(qp)
    qm = (jnp.where(first, qp, zero), jnp.where(first, zero, qp))

    def block(j, masked):
        start = pl.multiple_of(j * T, T)
        ks = k_ref[pl.ds(start, T), :]
        vs = v_ref[pl.ds(start, T), :]
        if masked:
            row_i = lax.broadcasted_iota(jnp.int32, (T, T), 0)
            col_i = lax.broadcasted_iota(jnp.int32, (T, T), 1)
            strict = col_i < row_i
        for hh in range(2):
            z = _dot_nt(qm[hh], ks)
            w = jnp.maximum(z, 0.0) + jnp.log(1.0 + jnp.exp2(_neg_abs(z))) * LOG2E
            if masked:
                w = jnp.where(strict, w, 0.0)
            parts = [None] * (T // G)
            for g in reversed(range(T // G)):
                gs = slice(g * G, (g + 1) * G)
                wg = w[:, gs]
                res = _dot(wg.astype(BF16), tt_ref[...])
                carry = carry_ref[hh]
                a = jnp.exp2((res + z[:, gs]) + jnp.concatenate([carry] * (G // LANES), axis=1))
                carry_ref[hh] = carry - jnp.sum(wg, axis=1, keepdims=True)
                if masked:
                    a = jnp.where(strict[:, gs], a, 0.0)
                parts[g] = a.astype(BF16)
            acc_ref[hh] += _dot(jnp.concatenate(parts, axis=1), vs)

    carry_ref[...] = jnp.zeros_like(carry_ref)
    acc_ref[...] = jnp.zeros_like(acc_ref)
    block(qi, True)

    def body(i, c):
        block(qi - 1 - i, False)
        return c

    lax.fori_loop(0, qi, body, 0)
    out_ref[...] = jnp.where(first, acc_ref[0], acc_ref[1]).astype(BF16)


def _stick_breaking(qs, ks, vs, tt):
    B, S, _ = qs.shape
    T = SB_TILE
    return pl.pallas_call(
        _sb_kernel,
        grid=(B, SB_WIDTH // LANES, S // T),
        in_specs=[pl.BlockSpec((None, T, LANES), lambda b, p, i: (b, i, p)),
                  pl.BlockSpec((None, S, LANES), lambda b, p, i: (b, 0, p)),
                  pl.BlockSpec((None, S, LANES), lambda b, p, i: (b, 0, p)),
                  _const_spec((SB_GROUP, SB_GROUP))],
        out_specs=pl.BlockSpec((None, T, LANES), lambda b, p, i: (b, i, p)),
        out_shape=jax.ShapeDtypeStruct((B, S, SB_WIDTH), BF16),
        scratch_shapes=[pltpu.VMEM((2, T, LANES), F32), pltpu.VMEM((2, T, LANES), F32)],
        compiler_params=pltpu.CompilerParams(
            dimension_semantics=("parallel", "parallel", "arbitrary"),
            vmem_limit_bytes=VMEM_LIMIT),
        name="stickbreak",
    )(qs, ks, vs, tt)


def _merge_kernel(x_ref, ya_ref, yb_ref, g_ref, wa_ref, wb_ref, wo_ref, gffn_ref, xo_ref, h2_ref):
    pa = _dot(ya_ref[...], wa_ref[...])
    pb = _dot(yb_ref[...], wb_ref[...])
    mix = g_ref[:, :D_MODEL].astype(F32) * pa + g_ref[:, D_MODEL:].astype(F32) * pb
    xn = x_ref[...] + _dot(mix.astype(BF16), wo_ref[...])
    xo_ref[...] = xn
    ms = jnp.mean(xn * xn, axis=-1, keepdims=True)
    h2_ref[...] = (xn * lax.rsqrt(ms + EPS) * gffn_ref[...]).astype(BF16)


def _merge(x2, ya, yb, g, wa, wb, wo, gffn):
    T = x2.shape[0]
    tm = ROW_TILE
    row = lambda n: pl.BlockSpec((tm, n), lambda i: (i, 0))
    return pl.pallas_call(
        _merge_kernel,
        grid=(T // tm,),
        in_specs=[row(D_MODEL), row(ML_WIDTH), row(SB_WIDTH), row(N_BRANCH * D_MODEL),
                  _const_spec((ML_WIDTH, D_MODEL)), _const_spec((SB_WIDTH, D_MODEL)),
                  _const_spec((D_MODEL, D_MODEL)), _const_spec((1, D_MODEL))],
        out_specs=(row(D_MODEL), row(D_MODEL)),
        out_shape=(jax.ShapeDtypeStruct((T, D_MODEL), F32), jax.ShapeDtypeStruct((T, D_MODEL), BF16)),
        compiler_params=pltpu.CompilerParams(dimension_semantics=("parallel",),
                                             vmem_limit_bytes=VMEM_LIMIT),
        name="merge",
    )(x2, ya, yb, g, wa, wb, wo, gffn)


def _ffn_kernel(x_ref, h2_ref, wgu_ref, wd_ref, out_ref):
    h2 = h2_ref[...]
    acc = x_ref[...]
    for c in range(D_FF // FF_CHUNK):
        lo = c * FF_CHUNK
        gt = _dot(h2, wgu_ref[:, lo:lo + FF_CHUNK])
        up = _dot(h2, wgu_ref[:, D_FF + lo:D_FF + lo + FF_CHUNK])
        act = (gt * _sigmoid(gt) * up).astype(BF16)
        acc = acc + _dot(act, wd_ref[lo:lo + FF_CHUNK, :])
    out_ref[...] = acc


def _ffn(x2, h2, wgu, wd):
    T = x2.shape[0]
    tm = ROW_TILE
    row = lambda n: pl.BlockSpec((tm, n), lambda i: (i, 0))
    return pl.pallas_call(
        _ffn_kernel,
        grid=(T // tm,),
        in_specs=[row(D_MODEL), row(D_MODEL), _const_spec((D_MODEL, 2 * D_FF)),
                  _const_spec((D_FF, D_MODEL))],
        out_specs=row(D_MODEL),
        out_shape=jax.ShapeDtypeStruct((T, D_MODEL), F32),
        compiler_params=pltpu.CompilerParams(dimension_semantics=("parallel",),
                                             vmem_limit_bytes=VMEM_LIMIT),
        name="ffn",
    )(x2, h2, wgu, wd)


def _constants():
    L = ML_CHUNK
    r = np.arange(L)
    tril = (r[None, :] <= r[:, None]).astype(np.float32)
    g = np.arange(SB_WIDTH) // SB_HEAD_DIM
    bd = (g[:, None] == g[None, :]).astype(np.float32)
    s = np.arange(SB_GROUP)
    tt = -(s[:, None] >= s[None, :]).astype(np.float32)
    return (jnp.asarray(tril, BF16), jnp.asarray(tril.T, BF16), jnp.asarray(bd, BF16),
            jnp.asarray(tt, BF16))


def kernel(x, g_mix, w_in, conv_w, conv_b, b_gates, g_q, g_k, w_br_a, w_br_b, w_out, g_ffn, w_gu,
           w_down):
    B, S, D = x.shape
    depth = g_mix.shape[0]
    T = B * S
    tril, triu, bd, tt = _constants()
    n_gate = 2 * ML_HEADS
    o_gates = 2 * ML_WIDTH + 2 * ML_WIDTH

    x2 = x.reshape(T, D)
    for l in range(depth):
        w = w_in[l]
        w_perm = jnp.concatenate(
            [w[:, :o_gates], w[:, o_gates + n_gate:],
             jnp.pad(w[:, o_gates:o_gates + n_gate], ((0, 0), (0, GATE_PAD - n_gate)))],
            axis=1).astype(BF16)
        bg = jnp.pad(b_gates[l], (0, GATE_PAD - n_gate)).reshape(1, GATE_PAD)
        gq = jnp.tile(g_q[l], SB_HEADS).reshape(1, SB_WIDTH)
        gk = jnp.tile(g_k[l], SB_HEADS).reshape(1, SB_WIDTH)

        qk_pre, vm, om, qs, ks, vs, gp, gates = _inproj(
            x2, g_mix[l].reshape(1, D), w_perm, bg, gq, gk, bd)

        gcol = gates[:, :n_gate].reshape(B, S, n_gate)
        grow = jnp.pad(jnp.swapaxes(gcol, 1, 2), ((0, 0), (0, 2 * SUBLANES - n_gate), (0, 0)))
        ya = _mlstm(qk_pre.reshape(B, S, -1), vm.reshape(B, S, -1), om.reshape(B, S, -1),
                    gcol, grow, conv_w[l], conv_b[l].reshape(1, -1), tril, triu)
        yb = _stick_breaking(qs.reshape(B, S, -1), ks.reshape(B, S, -1), vs.reshape(B, S, -1), tt)

        x2, h2 = _merge(x2, ya.reshape(T, -1), yb.reshape(T, -1), gp,
                        w_br_a[l].astype(BF16), w_br_b[l].astype(BF16), w_out[l].astype(BF16),
                        g_ffn[l].reshape(1, D))
        x2 = _ffn(x2, h2, w_gu[l].astype(BF16), w_down[l].astype(BF16))
    return x2.reshape(B, S, D)
```

```python
import functools

import jax
import jax.numpy as jnp
import numpy as np
from jax import lax
from jax.experimental import pallas as pl
from jax.experimental.pallas import tpu as pltpu

D_MODEL = 1024
ML_HEADS = 4
ML_HEAD_DIM = 128
ML_WIDTH = ML_HEADS * ML_HEAD_DIM
CONV_WIDTH = 4
SB_HEADS = 8
SB_HEAD_DIM = 64
SB_WIDTH = SB_HEADS * SB_HEAD_DIM
N_BRANCH = 2
D_FF = 2816
EPS = 1e-6

LANES = 128
SUBLANES = 8
GATE_PAD = LANES
ROW_TILE = 512
ML_CHUNK = 256
SB_TILE = 512
SB_GROUP = 256
LOG2E = 1.4426950408889634
EXP2_MAX = 126.0
FF_CHUNK = 256
VMEM_LIMIT = 56 * 1024 * 1024

F32 = jnp.float32
BF16 = jnp.bfloat16

C_QK = 0
C_VM = C_QK + 2 * ML_WIDTH
C_OM = C_VM + ML_WIDTH
C_QS = C_OM + ML_WIDTH
C_KS = C_QS + SB_WIDTH
C_VS = C_KS + SB_WIDTH
C_GP = C_VS + SB_WIDTH
C_GATES = C_GP + N_BRANCH * D_MODEL
C_END = C_GATES + GATE_PAD


def _const_spec(shape):
    nd = len(shape)
    return pl.BlockSpec(shape, lambda *_: (0,) * nd, pipeline_mode=pl.Buffered(1))


def _split3(a):
    hi = a.astype(BF16)
    r = a - hi.astype(F32)
    mid = r.astype(BF16)
    lo = (r - mid.astype(F32)).astype(BF16)
    return hi, mid, lo


def _dot(a, b):
    return jnp.dot(a, b, preferred_element_type=F32)


def _dot_nt(a, b):
    return lax.dot_general(a, b, (((1,), (1,)), ((), ())), preferred_element_type=F32)


def _sigmoid(x):
    return 1.0 / (1.0 + jnp.exp(-x))


def _inproj_kernel(x_ref, gmix_ref, w_ref, bg_ref, gq_ref, gk_ref, bd_ref,
                   qk_ref, vm_ref, om_ref, qs_ref, ks_ref, vs_ref, gp_ref, gates_ref):
    x = x_ref[...]
    ms = jnp.mean(x * x, axis=-1, keepdims=True)
    h = (x * lax.rsqrt(ms + EPS) * gmix_ref[...]).astype(BF16)

    def proj(a, b):
        return _dot(h, w_ref[:, a:b])

    def head_rms(y, g):
        gs = _dot((y * y).astype(BF16), bd_ref[...])
        return y * lax.rsqrt(gs * (1.0 / SB_HEAD_DIM) + EPS) * g

    qk_ref[...] = proj(C_QK, C_VM)
    vm_ref[...] = proj(C_VM, C_OM).astype(BF16)
    om_ref[...] = _sigmoid(proj(C_OM, C_QS)).astype(BF16)
    qs_ref[...] = (head_rms(proj(C_QS, C_KS), gq_ref[...]) * (SB_HEAD_DIM ** -0.5 * LOG2E)).astype(BF16)
    ks_ref[...] = head_rms(proj(C_KS, C_VS), gk_ref[...]).astype(BF16)
    vs_ref[...] = proj(C_VS, C_GP).astype(BF16)
    gp_ref[...] = _sigmoid(proj(C_GP, C_GATES)).astype(BF16)
    gates_ref[...] = proj(C_GATES, C_END) + bg_ref[...]


def _inproj(x2, gmix, w_perm, bg, gq, gk, bd):
    T = x2.shape[0]
    tm = ROW_TILE
    row = lambda n: pl.BlockSpec((tm, n), lambda i: (i, 0))
    out_shapes = (
        jax.ShapeDtypeStruct((T, 2 * ML_WIDTH), F32),
        jax.ShapeDtypeStruct((T, ML_WIDTH), BF16),
        jax.ShapeDtypeStruct((T, ML_WIDTH), BF16),
        jax.ShapeDtypeStruct((T, SB_WIDTH), BF16),
        jax.ShapeDtypeStruct((T, SB_WIDTH), BF16),
        jax.ShapeDtypeStruct((T, SB_WIDTH), BF16),
        jax.ShapeDtypeStruct((T, N_BRANCH * D_MODEL), BF16),
        jax.ShapeDtypeStruct((T, GATE_PAD), F32),
    )
    return pl.pallas_call(
        _inproj_kernel,
        grid=(T // tm,),
        in_specs=[row(D_MODEL), _const_spec((1, D_MODEL)), _const_spec((D_MODEL, C_END)),
                  _const_spec((1, GATE_PAD)), _const_spec((1, SB_WIDTH)), _const_spec((1, SB_WIDTH)),
                  _const_spec((SB_WIDTH, SB_WIDTH))],
        out_specs=(row(2 * ML_WIDTH), row(ML_WIDTH), row(ML_WIDTH), row(SB_WIDTH), row(SB_WIDTH),
                   row(SB_WIDTH), row(N_BRANCH * D_MODEL), row(GATE_PAD)),
        out_shape=out_shapes,
        compiler_params=pltpu.CompilerParams(dimension_semantics=("parallel",),
                                             vmem_limit_bytes=VMEM_LIMIT),
        name="inproj",
    )(x2, gmix, w_perm, bg, gq, gk, bd)


def _log_sigmoid(x):
    return jnp.minimum(x, 0.0) - jnp.log1p(jnp.exp(-jnp.abs(x)))


def _mlstm_kernel(qk_ref, vm_ref, om_ref, gcol_ref, grow_ref, cw_ref, cb_ref, tril_ref, triu_ref,
                  out_ref, xbuf, c_ref, m_ref):
    L = ML_CHUNK
    H = ML_HEADS
    Dh = ML_HEAD_DIM
    halo = SUBLANES

    @pl.when(pl.program_id(1) == 0)
    def _():
        xbuf[0:halo, :] = jnp.zeros((halo, 2 * ML_WIDTH), F32)
        c_ref[...] = jnp.zeros_like(c_ref)
        m_ref[...] = jnp.zeros_like(m_ref)

    xbuf[halo:halo + L, :] = qk_ref[...]
    acc = jnp.broadcast_to(cb_ref[...], (L, 2 * ML_WIDTH))
    for j in range(CONV_WIDTH):
        off = halo - (CONV_WIDTH - 1) + j
        acc = acc + cw_ref[j:j + 1, :] * xbuf[off:off + L, :]
    xbuf[0:halo, :] = xbuf[L:L + halo, :]
    qkc = acc * _sigmoid(acc)
    q_all = qkc[:, :ML_WIDTH]
    k_all = qkc[:, ML_WIDTH:] * (Dh ** -0.5)

    gc = gcol_ref[...]
    gr = grow_ref[...]
    lfc = _log_sigmoid(gc)
    lfr = _log_sigmoid(gr)
    bcol = sum(_dot(tril_ref[...], p) for p in _split3(lfc))
    brow = sum(_dot(p, triu_ref[...]) for p in _split3(lfr))

    row_i = lax.broadcasted_iota(jnp.int32, (L, L), 0)
    col_i = lax.broadcasted_iota(jnp.int32, (L, L), 1)
    causal = col_i <= row_i
    ones_aug = jnp.ones((L, Dh), BF16)

    for h in range(H):
        bc = bcol[:, H + h:H + h + 1]
        br = brow[H + h:H + h + 1, :]
        ir = gr[h:h + 1, :]
        ic = gc[:, h:h + 1]
        mprev = m_ref[h:h + 1, 0:1]

        logd = jnp.where(causal, bc - br + ir, -jnp.inf)
        mt = jnp.maximum(bc + mprev, jnp.max(logd, axis=-1, keepdims=True))
        w_intra = jnp.exp(logd - mt)
        w_inter = jnp.exp(bc + mprev - mt)

        sl = slice(h * Dh, (h + 1) * Dh)
        qh = q_all[:, sl].astype(BF16)
        kf = k_all[:, sl]
        kh = kf.astype(BF16)
        vaug = jnp.concatenate([vm_ref[:, sl], ones_aug], axis=1)
        caug = c_ref[h]

        s = _dot_nt(qh, kh) * w_intra
        u = w_inter * _dot(qh, caug.astype(BF16)) + _dot(s.astype(BF16), vaug)
        num = u[:, :Dh]
        den = u[:, Dh:]
        hout = num / jnp.maximum(jnp.abs(den), jnp.exp(-mt))
        out_ref[:, sl] = (om_ref[:, sl].astype(F32) * hout).astype(BF16)

        blast = bc[L - 1:L, :]
        logw_c = blast - bc + ic
        logw_r = blast - br + ir
        mnew = jnp.maximum(blast + mprev, jnp.max(logw_r, axis=-1, keepdims=True))
        w_c = jnp.exp(logw_c - mnew)
        decay = jnp.exp(blast + mprev - mnew)
        kwt = (kf * w_c).T.astype(BF16)
        c_ref[h] = decay * caug + _dot(kwt, vaug)
        m_ref[h:h + 1, :] = jnp.broadcast_to(mnew, (1, LANES))


def _mlstm(qk_pre, vm, om, gcol, grow, conv_w, conv_b, tril, triu):
    B, S, _ = qk_pre.shape
    L = ML_CHUNK
    blk = lambda n: pl.BlockSpec((None, L, n), lambda b, c: (b, c, 0))
    return pl.pallas_call(
        _mlstm_kernel,
        grid=(B, S // L),
        in_specs=[blk(2 * ML_WIDTH), blk(ML_WIDTH), blk(ML_WIDTH), blk(2 * ML_HEADS),
                  pl.BlockSpec((None, 2 * SUBLANES, L), lambda b, c: (b, 0, c)),
                  _const_spec((CONV_WIDTH, 2 * ML_WIDTH)), _const_spec((1, 2 * ML_WIDTH)),
                  _const_spec((L, L)), _const_spec((L, L))],
        out_specs=blk(ML_WIDTH),
        out_shape=jax.ShapeDtypeStruct((B, S, ML_WIDTH), BF16),
        scratch_shapes=[pltpu.VMEM((L + SUBLANES, 2 * ML_WIDTH), F32),
                        pltpu.VMEM((ML_HEADS, ML_HEAD_DIM, 2 * ML_HEAD_DIM), F32),
                        pltpu.VMEM((SUBLANES, LANES), F32)],
        compiler_params=pltpu.CompilerParams(dimension_semantics=("parallel", "arbitrary"),
                                             vmem_limit_bytes=VMEM_LIMIT),
        name="mlstm",
    )(qk_pre, vm, om, gcol, grow, conv_w, conv_b, tril, triu)


def _sb_kernel(q_ref, k_ref, v_ref, tt_ref, out_ref, carry_ref, acc_ref, z_ref, zd_ref, a_ref, qm_ref):
    T = SB_TILE
    G = SB_GROUP
    NG = T // G
    NT = q_ref.shape[0] // T
    lane = lax.broadcasted_iota(jnp.int32, (1, LANES), 1)
    first = lane < SB_HEAD_DIM

    def tile(ref, j):
        return ref[pl.ds(pl.multiple_of(j * T, T), T), :]

    def masked_q(j):
        qp = tile(q_ref, j)
        zero = jnp.zeros_like(qp)
        return jnp.where(first, qp, zero), jnp.where(first, zero, qp)

    def scores(qh, kblk, g):
        return _dot_nt(qh, kblk[g * G:(g + 1) * G, :])

    def weights(hh, r0, r1, g, z, diagonal):
        w = jnp.maximum(z, jnp.log(1.0 + jnp.exp2(jnp.minimum(z, EXP2_MAX))) * LOG2E)
        if diagonal:
            row_i = lax.broadcasted_iota(jnp.int32, (r1 - r0, G), 0)
            col_i = lax.broadcasted_iota(jnp.int32, (r1 - r0, G), 1)
            strict = col_i < row_i
            w = jnp.where(strict, w, 0.0)
        arg = _dot(w.astype(BF16), tt_ref[...]) + z
        rowsum = jnp.sum(w, axis=1, keepdims=True)
        if diagonal:
            carry_ref[hh, r0:r1, :] = jnp.broadcast_to(-rowsum, (r1 - r0, LANES))
            a = jnp.where(strict, jnp.exp2(arg), 0.0)
        else:
            carry = carry_ref[hh, r0:r1, :]
            carry_ref[hh, r0:r1, :] = carry - rowsum
            a = jnp.exp2(arg + jnp.concatenate([carry] * (G // LANES), axis=1))
        a_ref[hh, r0:r1, g * G:(g + 1) * G] = a.astype(BF16)

    def finish(j):
        v_last = v_ref[0:T, :]
        out = [acc_ref[hh] + _dot(a_ref[hh], v_last) for hh in range(2)]
        res = jnp.where(first, out[0], out[1])
        out_ref[pl.ds(pl.multiple_of(j * T, T), T), :] = res.astype(BF16)
        bits = lax.bitcast_convert_type(res, jnp.uint32)
        return lax.bitcast_convert_type((bits >> 16) >> 16, F32)

    acc_ref[...] = jnp.zeros_like(acc_ref)
    a_ref[...] = jnp.zeros_like(a_ref)
    q0 = masked_q(0)
    k0 = k_ref[0:T, :]
    for hh in range(2):
        for g in range(NG):
            zd_ref[hh, :, g * G:(g + 1) * G] = scores(q0[hh], k0, g)

    def tile_body(qi, c):
        zeros = finish(jnp.maximum(qi - 1, 0))
        zeros_q = jnp.concatenate([zeros[0:G]] * (G // LANES), axis=1).astype(BF16)
        for hh in range(2):
            acc_ref[hh] = zeros
        qm = masked_q(qi)
        qn = masked_q(jnp.minimum(qi + 1, NT - 1))
        kd_next = tile(k_ref, jnp.minimum(qi + 1, NT - 1))
        k_next = tile(k_ref, jnp.maximum(qi - 1, 0))
        for hh in range(2):
            qm_ref[hh] = qm[hh]
        for r in range(NG):
            r0, r1 = r * G, (r + 1) * G
            for g in reversed(range(NG)):
                gs = slice(g * G, (g + 1) * G)
                for hh in range(2):
                    if g > r:
                        a_ref[hh, r0:r1, gs] = zeros_q
                        continue
                    weights(hh, r0, r1, g, zd_ref[hh, r0:r1, gs], g == r)
                    zd_ref[hh, r0:r1, gs] = scores(qn[hh][r0:r1], kd_next, g)
        for g in range(NG):
            for hh in range(2):
                z_ref[hh, :, g * G:(g + 1) * G] = scores(qm[hh], k_next, g)

        def block_body(i, c2):
            j = qi - 1 - i
            v_prev = tile(v_ref, j + 1)
            k_nxt = tile(k_ref, jnp.maximum(j - 1, 0))
            for g in reversed(range(NG)):
                gs = slice(g * G, (g + 1) * G)
                for hh in range(2):
                    acc_ref[hh] += _dot(a_ref[hh, :, gs], v_prev[gs, :])
                    weights(hh, 0, T, g, z_ref[hh, :, gs], False)
                    z_ref[hh, :, gs] = scores(qm_ref[hh], k_nxt, g)
            return c2

        lax.fori_loop(0, qi, block_body, 0)
        return c

    lax.fori_loop(0, NT, tile_body, 0)
    finish(NT - 1)


def _stick_breaking(qs, ks, vs, tt):
    B, S, _ = qs.shape
    T = SB_TILE
    seq = pl.BlockSpec((None, S, LANES), lambda b, p: (b, 0, p))
    return pl.pallas_call(
        _sb_kernel,
        grid=(B, SB_WIDTH // LANES),
        in_specs=[seq, seq, seq, _const_spec((SB_GROUP, SB_GROUP))],
        out_specs=seq,
        out_shape=jax.ShapeDtypeStruct((B, S, SB_WIDTH), BF16),
        scratch_shapes=[pltpu.VMEM((2, T, LANES), F32), pltpu.VMEM((2, T, LANES), F32),
                        pltpu.VMEM((2, T, T), F32), pltpu.VMEM((2, T, T), F32),
                        pltpu.VMEM((2, T, T), BF16), pltpu.VMEM((2, T, LANES), BF16)],
        compiler_params=pltpu.CompilerParams(
            dimension_semantics=("parallel", "parallel"),
            vmem_limit_bytes=VMEM_LIMIT),
        name="stickbreak",
    )(qs, ks, vs, tt)


def _merge_kernel(x_ref, ya_ref, yb_ref, g_ref, wa_ref, wb_ref, wo_ref, gffn_ref, xo_ref, h2_ref):
    pa = _dot(ya_ref[...], wa_ref[...])
    pb = _dot(yb_ref[...], wb_ref[...])
    mix = g_ref[:, :D_MODEL].astype(F32) * pa + g_ref[:, D_MODEL:].astype(F32) * pb
    xn = x_ref[...] + _dot(mix.astype(BF16), wo_ref[...])
    xo_ref[...] = xn
    ms = jnp.mean(xn * xn, axis=-1, keepdims=True)
    h2_ref[...] = (xn * lax.rsqrt(ms + EPS) * gffn_ref[...]).astype(BF16)


def _merge(x2, ya, yb, g, wa, wb, wo, gffn):
    T = x2.shape[0]
    tm = ROW_TILE
    row = lambda n: pl.BlockSpec((tm, n), lambda i: (i, 0))
    return pl.pallas_call(
        _merge_kernel,
        grid=(T // tm,),
        in_specs=[row(D_MODEL), row(ML_WIDTH), row(SB_WIDTH), row(N_BRANCH * D_MODEL),
                  _const_spec((ML_WIDTH, D_MODEL)), _const_spec((SB_WIDTH, D_MODEL)),
                  _const_spec((D_MODEL, D_MODEL)), _const_spec((1, D_MODEL))],
        out_specs=(row(D_MODEL), row(D_MODEL)),
        out_shape=(jax.ShapeDtypeStruct((T, D_MODEL), F32), jax.ShapeDtypeStruct((T, D_MODEL), BF16)),
        compiler_params=pltpu.CompilerParams(dimension_semantics=("parallel",),
                                             vmem_limit_bytes=VMEM_LIMIT),
        name="merge",
    )(x2, ya, yb, g, wa, wb, wo, gffn)


def _ffn_kernel(x_ref, h2_ref, wgu_ref, wd_ref, out_ref):
    h2 = h2_ref[...]
    acc = x_ref[...]
    for c in range(D_FF // FF_CHUNK):
        lo = c * FF_CHUNK
        gt = _dot(h2, wgu_ref[:, lo:lo + FF_CHUNK])
        up = _dot(h2, wgu_ref[:, D_FF + lo:D_FF + lo + FF_CHUNK])
        act = (gt * _sigmoid(gt) * up).astype(BF16)
        acc = acc + _dot(act, wd_ref[lo:lo + FF_CHUNK, :])
    out_ref[...] = acc


def _ffn(x2, h2, wgu, wd):
    T = x2.shape[0]
    tm = ROW_TILE
    row = lambda n: pl.BlockSpec((tm, n), lambda i: (i, 0))
    return pl.pallas_call(
        _ffn_kernel,
        grid=(T // tm,),
        in_specs=[row(D_MODEL), row(D_MODEL), _const_spec((D_MODEL, 2 * D_FF)),
                  _const_spec((D_FF, D_MODEL))],
        out_specs=row(D_MODEL),
        out_shape=jax.ShapeDtypeStruct((T, D_MODEL), F32),
        compiler_params=pltpu.CompilerParams(dimension_semantics=("parallel",),
                                             vmem_limit_bytes=VMEM_LIMIT),
        name="ffn",
    )(x2, h2, wgu, wd)


def _constants():
    L = ML_CHUNK
    r = np.arange(L)
    tril = (r[None, :] <= r[:, None]).astype(np.float32)
    g = np.arange(SB_WIDTH) // SB_HEAD_DIM
    bd = (g[:, None] == g[None, :]).astype(np.float32)
    s = np.arange(SB_GROUP)
    tt = -(s[:, None] >= s[None, :]).astype(np.float32)
    return (jnp.asarray(tril, BF16), jnp.asarray(tril.T, BF16), jnp.asarray(bd, BF16),
            jnp.asarray(tt, BF16))


def kernel(x, g_mix, w_in, conv_w, conv_b, b_gates, g_q, g_k, w_br_a, w_br_b, w_out, g_ffn, w_gu,
           w_down):
    B, S, D = x.shape
    depth = g_mix.shape[0]
    T = B * S
    tril, triu, bd, tt = _constants()
    n_gate = 2 * ML_HEADS
    o_gates = 2 * ML_WIDTH + 2 * ML_WIDTH

    x2 = x.reshape(T, D)
    for l in range(depth):
        w = w_in[l]
        w_perm = jnp.concatenate(
            [w[:, :o_gates], w[:, o_gates + n_gate:],
             jnp.pad(w[:, o_gates:o_gates + n_gate], ((0, 0), (0, GATE_PAD - n_gate)))],
            axis=1).astype(BF16)
        bg = jnp.pad(b_gates[l], (0, GATE_PAD - n_gate)).reshape(1, GATE_PAD)
        gq = jnp.tile(g_q[l], SB_HEADS).reshape(1, SB_WIDTH)
        gk = jnp.tile(g_k[l], SB_HEADS).reshape(1, SB_WIDTH)

        qk_pre, vm, om, qs, ks, vs, gp, gates = _inproj(
            x2, g_mix[l].reshape(1, D), w_perm, bg, gq, gk, bd)

        gcol = gates[:, :n_gate].reshape(B, S, n_gate)
        grow = jnp.pad(jnp.swapaxes(gcol, 1, 2), ((0, 0), (0, 2 * SUBLANES - n_gate), (0, 0)))
        ya = _mlstm(qk_pre.reshape(B, S, -1), vm.reshape(B, S, -1), om.reshape(B, S, -1),
                    gcol, grow, conv_w[l], conv_b[l].reshape(1, -1), tril, triu)
        yb = _stick_breaking(qs.reshape(B, S, -1), ks.reshape(B, S, -1), vs.reshape(B, S, -1), tt)

        x2, h2 = _merge(x2, ya.reshape(T, -1), yb.reshape(T, -1), gp,
                        w_br_a[l].astype(BF16), w_br_b[l].astype(BF16), w_out[l].astype(BF16),
                        g_ffn[l].reshape(1, D))
        x2 = _ffn(x2, h2, w_gu[l].astype(BF16), w_down[l].astype(BF16))
    return x2.reshape(B, S, D)
```

```python
import jax
import jax.numpy as jnp
import numpy as np
from jax import lax
from jax.experimental import pallas as pl
from jax.experimental.pallas import tpu as pltpu

D_MODEL = 1024
ML_HEADS = 4
ML_HEAD_DIM = 128
ML_WIDTH = ML_HEADS * ML_HEAD_DIM
CONV_WIDTH = 4
SB_HEADS = 8
SB_HEAD_DIM = 64
SB_WIDTH = SB_HEADS * SB_HEAD_DIM
N_BRANCH = 2
D_FF = 2816
EPS = 1e-6

LANES = 128
SUBLANES = 8
GATE_PAD = LANES
ROW_TILE = 512
ML_CHUNK = 256
SB_TILE = 512
SB_GROUP = 256
LOG2E = 1.4426950408889634
EXP2_MAX = 126.0
SB_UNDERFLOW = -160.0
FF_CHUNK = 256
VMEM_LIMIT = 56 * 1024 * 1024

F32 = jnp.float32
BF16 = jnp.bfloat16

C_QK = 0
C_VM = C_QK + 2 * ML_WIDTH
C_OM = C_VM + ML_WIDTH
C_QS = C_OM + ML_WIDTH
C_KS = C_QS + SB_WIDTH
C_VS = C_KS + SB_WIDTH
C_GP = C_VS + SB_WIDTH
C_GATES = C_GP + N_BRANCH * D_MODEL
C_END = C_GATES + GATE_PAD


def _const_spec(shape):
    nd = len(shape)
    return pl.BlockSpec(shape, lambda *_: (0,) * nd, pipeline_mode=pl.Buffered(1))


def _split3(a):
    hi = a.astype(BF16)
    r = a - hi.astype(F32)
    mid = r.astype(BF16)
    lo = (r - mid.astype(F32)).astype(BF16)
    return hi, mid, lo


def _dot(a, b):
    return jnp.dot(a, b, preferred_element_type=F32)


def _dot_nt(a, b):
    return lax.dot_general(a, b, (((1,), (1,)), ((), ())), preferred_element_type=F32)


def _sigmoid(x):
    return 1.0 / (1.0 + jnp.exp(-x))


def _inproj_kernel(x_ref, gmix_ref, w_ref, bg_ref, gq_ref, gk_ref, bd_ref,
                   qk_ref, vm_ref, om_ref, qs_ref, ks_ref, vs_ref, gp_ref, gates_ref):
    x = x_ref[...]
    ms = jnp.mean(x * x, axis=-1, keepdims=True)
    h = (x * lax.rsqrt(ms + EPS) * gmix_ref[...]).astype(BF16)

    def proj(a, b):
        return _dot(h, w_ref[:, a:b])

    def head_rms(y, g):
        gs = _dot((y * y).astype(BF16), bd_ref[...])
        return y * lax.rsqrt(gs * (1.0 / SB_HEAD_DIM) + EPS) * g

    qk_ref[...] = proj(C_QK, C_VM)
    vm_ref[...] = proj(C_VM, C_OM).astype(BF16)
    om_ref[...] = _sigmoid(proj(C_OM, C_QS)).astype(BF16)
    qs_ref[...] = (head_rms(proj(C_QS, C_KS), gq_ref[...]) * (SB_HEAD_DIM ** -0.5 * LOG2E)).astype(BF16)
    ks_ref[...] = head_rms(proj(C_KS, C_VS), gk_ref[...]).astype(BF16)
    vs_ref[...] = proj(C_VS, C_GP).astype(BF16)
    gp_ref[...] = _sigmoid(proj(C_GP, C_GATES)).astype(BF16)
    gates_ref[...] = proj(C_GATES, C_END) + bg_ref[...]


def _inproj(x2, gmix, w_perm, bg, gq, gk, bd):
    T = x2.shape[0]
    tm = ROW_TILE
    row = lambda n: pl.BlockSpec((tm, n), lambda i: (i, 0))
    out_shapes = (
        jax.ShapeDtypeStruct((T, 2 * ML_WIDTH), F32),
        jax.ShapeDtypeStruct((T, ML_WIDTH), BF16),
        jax.ShapeDtypeStruct((T, ML_WIDTH), BF16),
        jax.ShapeDtypeStruct((T, SB_WIDTH), BF16),
        jax.ShapeDtypeStruct((T, SB_WIDTH), BF16),
        jax.ShapeDtypeStruct((T, SB_WIDTH), BF16),
        jax.ShapeDtypeStruct((T, N_BRANCH * D_MODEL), BF16),
        jax.ShapeDtypeStruct((T, GATE_PAD), F32),
    )
    return pl.pallas_call(
        _inproj_kernel,
        grid=(T // tm,),
        in_specs=[row(D_MODEL), _const_spec((1, D_MODEL)), _const_spec((D_MODEL, C_END)),
                  _const_spec((1, GATE_PAD)), _const_spec((1, SB_WIDTH)), _const_spec((1, SB_WIDTH)),
                  _const_spec((SB_WIDTH, SB_WIDTH))],
        out_specs=(row(2 * ML_WIDTH), row(ML_WIDTH), row(ML_WIDTH), row(SB_WIDTH), row(SB_WIDTH),
                   row(SB_WIDTH), row(N_BRANCH * D_MODEL), row(GATE_PAD)),
        out_shape=out_shapes,
        compiler_params=pltpu.CompilerParams(dimension_semantics=("parallel",),
                                             vmem_limit_bytes=VMEM_LIMIT),
        name="inproj",
    )(x2, gmix, w_perm, bg, gq, gk, bd)


def _log_sigmoid(x):
    return jnp.minimum(x, 0.0) - jnp.log1p(jnp.exp(-jnp.abs(x)))


def _mlstm_kernel(qk_ref, vm_ref, om_ref, gcol_ref, grow_ref, cw_ref, cb_ref, tril_ref, triu_ref,
                  out_ref, xbuf, c_ref, m_ref):
    L = ML_CHUNK
    H = ML_HEADS
    Dh = ML_HEAD_DIM
    halo = SUBLANES

    @pl.when(pl.program_id(1) == 0)
    def _():
        xbuf[0:halo, :] = jnp.zeros((halo, 2 * ML_WIDTH), F32)
        c_ref[...] = jnp.zeros_like(c_ref)
        m_ref[...] = jnp.zeros_like(m_ref)

    xbuf[halo:halo + L, :] = qk_ref[...]
    acc = jnp.broadcast_to(cb_ref[...], (L, 2 * ML_WIDTH))
    for j in range(CONV_WIDTH):
        off = halo - (CONV_WIDTH - 1) + j
        acc = acc + cw_ref[j:j + 1, :] * xbuf[off:off + L, :]
    xbuf[0:halo, :] = xbuf[L:L + halo, :]
    qkc = acc * _sigmoid(acc)
    q_all = qkc[:, :ML_WIDTH]
    k_all = qkc[:, ML_WIDTH:] * (Dh ** -0.5)

    gc = gcol_ref[...]
    gr = grow_ref[...]
    lfc = _log_sigmoid(gc)
    lfr = _log_sigmoid(gr)
    bcol = sum(_dot(tril_ref[...], p) for p in _split3(lfc))
    brow = sum(_dot(p, triu_ref[...]) for p in _split3(lfr))

    row_i = lax.broadcasted_iota(jnp.int32, (L, L), 0)
    col_i = lax.broadcasted_iota(jnp.int32, (L, L), 1)
    causal = col_i <= row_i
    ones_aug = jnp.ones((L, Dh), BF16)

    for h in range(H):
        bc = bcol[:, H + h:H + h + 1]
        br = brow[H + h:H + h + 1, :]
        ir = gr[h:h + 1, :]
        ic = gc[:, h:h + 1]
        mprev = m_ref[h:h + 1, 0:1]

        logd = jnp.where(causal, bc - br + ir, -jnp.inf)
        mt = jnp.maximum(bc + mprev, jnp.max(logd, axis=-1, keepdims=True))
        w_intra = jnp.exp(logd - mt)
        w_inter = jnp.exp(bc + mprev - mt)

        sl = slice(h * Dh, (h + 1) * Dh)
        qh = q_all[:, sl].astype(BF16)
        kf = k_all[:, sl]
        kh = kf.astype(BF16)
        vaug = jnp.concatenate([vm_ref[:, sl], ones_aug], axis=1)
        caug = c_ref[h]

        s = _dot_nt(qh, kh) * w_intra
        u = w_inter * _dot(qh, caug.astype(BF16)) + _dot(s.astype(BF16), vaug)
        num = u[:, :Dh]
        den = u[:, Dh:]
        hout = num / jnp.maximum(jnp.abs(den), jnp.exp(-mt))
        out_ref[:, sl] = (om_ref[:, sl].astype(F32) * hout).astype(BF16)

        blast = bc[L - 1:L, :]
        logw_c = blast - bc + ic
        logw_r = blast - br + ir
        mnew = jnp.maximum(blast + mprev, jnp.max(logw_r, axis=-1, keepdims=True))
        w_c = jnp.exp(logw_c - mnew)
        decay = jnp.exp(blast + mprev - mnew)
        kwt = (kf * w_c).T.astype(BF16)
        c_ref[h] = decay * caug + _dot(kwt, vaug)
        m_ref[h:h + 1, :] = jnp.broadcast_to(mnew, (1, LANES))


def _mlstm(qk_pre, vm, om, gcol, grow, conv_w, conv_b, tril, triu):
    B, S, _ = qk_pre.shape
    L = ML_CHUNK
    blk = lambda n: pl.BlockSpec((None, L, n), lambda b, c: (b, c, 0))
    return pl.pallas_call(
        _mlstm_kernel,
        grid=(B, S // L),
        in_specs=[blk(2 * ML_WIDTH), blk(ML_WIDTH), blk(ML_WIDTH), blk(2 * ML_HEADS),
                  pl.BlockSpec((None, 2 * SUBLANES, L), lambda b, c: (b, 0, c)),
                  _const_spec((CONV_WIDTH, 2 * ML_WIDTH)), _const_spec((1, 2 * ML_WIDTH)),
                  _const_spec((L, L)), _const_spec((L, L))],
        out_specs=blk(ML_WIDTH),
        out_shape=jax.ShapeDtypeStruct((B, S, ML_WIDTH), BF16),
        scratch_shapes=[pltpu.VMEM((L + SUBLANES, 2 * ML_WIDTH), F32),
                        pltpu.VMEM((ML_HEADS, ML_HEAD_DIM, 2 * ML_HEAD_DIM), F32),
                        pltpu.VMEM((SUBLANES, LANES), F32)],
        compiler_params=pltpu.CompilerParams(dimension_semantics=("parallel", "arbitrary"),
                                             vmem_limit_bytes=VMEM_LIMIT),
        name="mlstm",
    )(qk_pre, vm, om, gcol, grow, conv_w, conv_b, tril, triu)


def _sb_kernel(q_ref, k_ref, v_ref, tt_ref, out_ref, carry_ref, acc_ref, z_ref, zd_ref, a_ref, qm_ref):
    T = SB_TILE
    G = SB_GROUP
    NG = T // G
    NT = q_ref.shape[0] // T
    lane = lax.broadcasted_iota(jnp.int32, (1, LANES), 1)
    first = lane < SB_HEAD_DIM

    def tile(ref, j):
        return ref[pl.ds(pl.multiple_of(j * T, T), T), :]

    def masked_q(j):
        qp = tile(q_ref, j)
        zero = jnp.zeros_like(qp)
        return jnp.where(first, qp, zero), jnp.where(first, zero, qp)

    def scores(qh, kblk, g):
        return _dot_nt(qh, kblk[g * G:(g + 1) * G, :])

    def weights(hh, r0, r1, g, z, diagonal):
        w = jnp.maximum(z, jnp.log(1.0 + jnp.exp2(jnp.minimum(z, EXP2_MAX))) * LOG2E)
        if diagonal:
            row_i = lax.broadcasted_iota(jnp.int32, (r1 - r0, G), 0)
            col_i = lax.broadcasted_iota(jnp.int32, (r1 - r0, G), 1)
            strict = col_i < row_i
            w = jnp.where(strict, w, 0.0)
        arg = _dot(w.astype(BF16), tt_ref[...]) + z
        rowsum = jnp.sum(w, axis=1, keepdims=True)
        if diagonal:
            carry_ref[hh, r0:r1, :] = jnp.broadcast_to(-rowsum, (r1 - r0, LANES))
            a = jnp.where(strict, jnp.exp2(arg), 0.0)
        else:
            carry = carry_ref[hh, r0:r1, :]
            carry_ref[hh, r0:r1, :] = carry - rowsum
            a = jnp.exp2(arg + jnp.concatenate([carry] * (G // LANES), axis=1))
        a_ref[hh, r0:r1, g * G:(g + 1) * G] = a.astype(BF16)

    def finish(j, last):
        v_last = tile(v_ref, last)
        out = [acc_ref[hh] + _dot(a_ref[hh], v_last) for hh in range(2)]
        res = jnp.where(first, out[0], out[1])
        out_ref[pl.ds(pl.multiple_of(j * T, T), T), :] = res.astype(BF16)
        bits = lax.bitcast_convert_type(res, jnp.uint32)
        return lax.bitcast_convert_type((bits >> 16) >> 16, F32)

    acc_ref[...] = jnp.zeros_like(acc_ref)
    a_ref[...] = jnp.zeros_like(a_ref)
    q0 = masked_q(0)
    k0 = k_ref[0:T, :]
    for hh in range(2):
        for g in range(NG):
            zd_ref[hh, :, g * G:(g + 1) * G] = scores(q0[hh], k0, g)

    def tile_body(qi, last):
        zeros = finish(jnp.maximum(qi - 1, 0), last)
        zeros_q = jnp.concatenate([zeros[0:G]] * (G // LANES), axis=1).astype(BF16)
        for hh in range(2):
            acc_ref[hh] = zeros
        qm = masked_q(qi)
        qn = masked_q(jnp.minimum(qi + 1, NT - 1))
        kd_next = tile(k_ref, jnp.minimum(qi + 1, NT - 1))
        k_next = tile(k_ref, jnp.maximum(qi - 1, 0))
        for hh in range(2):
            qm_ref[hh] = qm[hh]
        for r in range(NG):
            r0, r1 = r * G, (r + 1) * G
            for g in reversed(range(NG)):
                gs = slice(g * G, (g + 1) * G)
                for hh in range(2):
                    if g > r:
                        a_ref[hh, r0:r1, gs] = zeros_q
                        continue
                    weights(hh, r0, r1, g, zd_ref[hh, r0:r1, gs], g == r)
                    zd_ref[hh, r0:r1, gs] = scores(qn[hh][r0:r1], kd_next, g)
        for g in range(NG):
            for hh in range(2):
                z_ref[hh, :, g * G:(g + 1) * G] = scores(qm[hh], k_next, g)

        def block_cond(state):
            i, live = state
            return jnp.logical_and(i < qi, live)

        def block_body(state):
            i, _ = state
            j = qi - 1 - i
            v_prev = tile(v_ref, j + 1)
            k_nxt = tile(k_ref, jnp.maximum(j - 1, 0))
            for g in reversed(range(NG)):
                gs = slice(g * G, (g + 1) * G)
                for hh in range(2):
                    acc_ref[hh] += _dot(a_ref[hh, :, gs], v_prev[gs, :])
                    weights(hh, 0, T, g, z_ref[hh, :, gs], False)
                    z_ref[hh, :, gs] = scores(qm_ref[hh], k_nxt, g)
            return i + 1, jnp.max(carry_ref[...]) >= SB_UNDERFLOW

        n_blocks, _ = lax.while_loop(block_cond, block_body, (jnp.int32(0), jnp.bool_(True)))
        return qi - n_blocks

    last = lax.fori_loop(0, NT, tile_body, jnp.int32(0))
    finish(NT - 1, last)


def _stick_breaking(qs, ks, vs, tt):
    B, S, _ = qs.shape
    T = SB_TILE
    seq = pl.BlockSpec((None, S, LANES), lambda b, p: (b, 0, p))
    return pl.pallas_call(
        _sb_kernel,
        grid=(B, SB_WIDTH // LANES),
        in_specs=[seq, seq, seq, _const_spec((SB_GROUP, SB_GROUP))],
        out_specs=seq,
        out_shape=jax.ShapeDtypeStruct((B, S, SB_WIDTH), BF16),
        scratch_shapes=[pltpu.VMEM((2, T, LANES), F32), pltpu.VMEM((2, T, LANES), F32),
                        pltpu.VMEM((2, T, T), F32), pltpu.VMEM((2, T, T), F32),
                        pltpu.VMEM((2, T, T), BF16), pltpu.VMEM((2, T, LANES), BF16)],
        compiler_params=pltpu.CompilerParams(
            dimension_semantics=("parallel", "parallel"),
            vmem_limit_bytes=VMEM_LIMIT),
        name="stickbreak",
    )(qs, ks, vs, tt)


def _mix_ffn_kernel(x_ref, ya_ref, yb_ref, g_ref, wa_ref, wb_ref, wo_ref, gffn_ref, wgu_ref, wd_ref,
                    out_ref):
    pa = _dot(ya_ref[...], wa_ref[...])
    pb = _dot(yb_ref[...], wb_ref[...])
    mix = g_ref[:, :D_MODEL].astype(F32) * pa + g_ref[:, D_MODEL:].astype(F32) * pb
    xn = x_ref[...] + _dot(mix.astype(BF16), wo_ref[...])
    ms = jnp.mean(xn * xn, axis=-1, keepdims=True)
    h2 = (xn * lax.rsqrt(ms + EPS) * gffn_ref[...]).astype(BF16)
    acc = xn
    for c in range(D_FF // FF_CHUNK):
        lo = c * FF_CHUNK
        gt = _dot(h2, wgu_ref[:, lo:lo + FF_CHUNK])
        up = _dot(h2, wgu_ref[:, D_FF + lo:D_FF + lo + FF_CHUNK])
        act = (gt * _sigmoid(gt) * up).astype(BF16)
        acc = acc + _dot(act, wd_ref[lo:lo + FF_CHUNK, :])
    out_ref[...] = acc


def _mix_ffn(x2, ya, yb, g, wa, wb, wo, gffn, wgu, wd):
    T = x2.shape[0]
    tm = ROW_TILE
    row = lambda n: pl.BlockSpec((tm, n), lambda i: (i, 0))
    return pl.pallas_call(
        _mix_ffn_kernel,
        grid=(T // tm,),
        in_specs=[row(D_MODEL), row(ML_WIDTH), row(SB_WIDTH), row(N_BRANCH * D_MODEL),
                  _const_spec((ML_WIDTH, D_MODEL)), _const_spec((SB_WIDTH, D_MODEL)),
                  _const_spec((D_MODEL, D_MODEL)), _const_spec((1, D_MODEL)),
                  _const_spec((D_MODEL, 2 * D_FF)), _const_spec((D_FF, D_MODEL))],
        out_specs=row(D_MODEL),
        out_shape=jax.ShapeDtypeStruct((T, D_MODEL), F32),
        compiler_params=pltpu.CompilerParams(dimension_semantics=("parallel",),
                                             vmem_limit_bytes=VMEM_LIMIT),
        name="mixffn",
    )(x2, ya, yb, g, wa, wb, wo, gffn, wgu, wd)


def _constants():
    L = ML_CHUNK
    r = np.arange(L)
    tril = (r[None, :] <= r[:, None]).astype(np.float32)
    g = np.arange(SB_WIDTH) // SB_HEAD_DIM
    bd = (g[:, None] == g[None, :]).astype(np.float32)
    s = np.arange(SB_GROUP)
    tt = -(s[:, None] >= s[None, :]).astype(np.float32)
    return (jnp.asarray(tril, BF16), jnp.asarray(tril.T, BF16), jnp.asarray(bd, BF16),
            jnp.asarray(tt, BF16))


def kernel(x, g_mix, w_in, conv_w, conv_b, b_gates, g_q, g_k, w_br_a, w_br_b, w_out, g_ffn, w_gu,
           w_down):
    B, S, D = x.shape
    depth = g_mix.shape[0]
    T = B * S
    tril, triu, bd, tt = _constants()
    n_gate = 2 * ML_HEADS
    o_gates = 2 * ML_WIDTH + 2 * ML_WIDTH

    x2 = x.reshape(T, D)
    for l in range(depth):
        w = w_in[l]
        w_perm = jnp.concatenate(
            [w[:, :o_gates], w[:, o_gates + n_gate:],
             jnp.pad(w[:, o_gates:o_gates + n_gate], ((0, 0), (0, GATE_PAD - n_gate)))],
            axis=1).astype(BF16)
        bg = jnp.pad(b_gates[l], (0, GATE_PAD - n_gate)).reshape(1, GATE_PAD)
        gq = jnp.tile(g_q[l], SB_HEADS).reshape(1, SB_WIDTH)
        gk = jnp.tile(g_k[l], SB_HEADS).reshape(1, SB_WIDTH)

        qk_pre, vm, om, qs, ks, vs, gp, gates = _inproj(
            x2, g_mix[l].reshape(1, D), w_perm, bg, gq, gk, bd)

        gcol = gates[:, :n_gate].reshape(B, S, n_gate)
        grow = jnp.pad(jnp.swapaxes(gcol, 1, 2), ((0, 0), (0, 2 * SUBLANES - n_gate), (0, 0)))
        ya = _mlstm(qk_pre.reshape(B, S, -1), vm.reshape(B, S, -1), om.reshape(B, S, -1),
                    gcol, grow, conv_w[l], conv_b[l].reshape(1, -1), tril, triu)
        yb = _stick_breaking(qs.reshape(B, S, -1), ks.reshape(B, S, -1), vs.reshape(B, S, -1), tt)

        x2 = _mix_ffn(x2, ya.reshape(T, -1), yb.reshape(T, -1), gp,
                      w_br_a[l].astype(BF16), w_br_b[l].astype(BF16), w_out[l].astype(BF16),
                      g_ffn[l].reshape(1, D), w_gu[l].astype(BF16), w_down[l].astype(BF16))
    return x2.reshape(B, S, D)
```

```python
import jax
import jax.numpy as jnp
import numpy as np
from jax import lax
from jax.experimental import pallas as pl
from jax.experimental.pallas import tpu as pltpu

D_MODEL = 1024
ML_HEADS = 4
ML_HEAD_DIM = 128
ML_WIDTH = ML_HEADS * ML_HEAD_DIM
CONV_WIDTH = 4
SB_HEADS = 8
SB_HEAD_DIM = 64
SB_WIDTH = SB_HEADS * SB_HEAD_DIM
N_BRANCH = 2
D_FF = 2816
EPS = 1e-6

LANES = 128
SUBLANES = 8
GATE_PAD = LANES
ROW_TILE = 512
ML_CHUNK = 256
SB_TILE = 256
SB_GROUP = 256
LOG2E = 1.4426950408889634
EXP2_MAX = 126.0
SB_UNDERFLOW = -160.0
FF_CHUNK = 256
VMEM_LIMIT = 56 * 1024 * 1024

F32 = jnp.float32
BF16 = jnp.bfloat16

C_QK = 0
C_VM = C_QK + 2 * ML_WIDTH
C_OM = C_VM + ML_WIDTH
C_QS = C_OM + ML_WIDTH
C_KS = C_QS + SB_WIDTH
C_VS = C_KS + SB_WIDTH
C_GP = C_VS + SB_WIDTH
C_GATES = C_GP + N_BRANCH * D_MODEL
C_END = C_GATES + GATE_PAD


def _const_spec(shape):
    nd = len(shape)
    return pl.BlockSpec(shape, lambda *_: (0,) * nd, pipeline_mode=pl.Buffered(1))


def _split3(a):
    hi = a.astype(BF16)
    r = a - hi.astype(F32)
    mid = r.astype(BF16)
    lo = (r - mid.astype(F32)).astype(BF16)
    return hi, mid, lo


def _dot(a, b):
    return jnp.dot(a, b, preferred_element_type=F32)


def _dot_nt(a, b):
    return lax.dot_general(a, b, (((1,), (1,)), ((), ())), preferred_element_type=F32)


def _sigmoid(x):
    return 0.5 * jnp.tanh(0.5 * x) + 0.5


def _inproj_kernel(x_ref, gmix_ref, w_ref, bg_ref, gq_ref, gk_ref, bd_ref,
                   qk_ref, vm_ref, om_ref, qs_ref, ks_ref, vs_ref, gp_ref, gates_ref):
    x = x_ref[...]
    ms = jnp.mean(x * x, axis=-1, keepdims=True)
    h = (x * lax.rsqrt(ms + EPS) * gmix_ref[...]).astype(BF16)

    def proj(a, b):
        return _dot(h, w_ref[:, a:b])

    def head_rms(y, g):
        gs = _dot((y * y).astype(BF16), bd_ref[...])
        return y * lax.rsqrt(gs * (1.0 / SB_HEAD_DIM) + EPS) * g

    qk_ref[...] = proj(C_QK, C_VM)
    vm_ref[...] = proj(C_VM, C_OM).astype(BF16)
    om_ref[...] = _sigmoid(proj(C_OM, C_QS)).astype(BF16)
    qs_ref[...] = (head_rms(proj(C_QS, C_KS), gq_ref[...]) * (SB_HEAD_DIM ** -0.5 * LOG2E)).astype(BF16)
    ks_ref[...] = head_rms(proj(C_KS, C_VS), gk_ref[...]).astype(BF16)
    vs_ref[...] = proj(C_VS, C_GP).astype(BF16)
    gp_ref[...] = _sigmoid(proj(C_GP, C_GATES)).astype(BF16)
    gates_ref[...] = proj(C_GATES, C_END) + bg_ref[...]


def _inproj(x2, gmix, w_perm, bg, gq, gk, bd):
    T = x2.shape[0]
    tm = ROW_TILE
    row = lambda n: pl.BlockSpec((tm, n), lambda i: (i, 0))
    out_shapes = (
        jax.ShapeDtypeStruct((T, 2 * ML_WIDTH), F32),
        jax.ShapeDtypeStruct((T, ML_WIDTH), BF16),
        jax.ShapeDtypeStruct((T, ML_WIDTH), BF16),
        jax.ShapeDtypeStruct((T, SB_WIDTH), BF16),
        jax.ShapeDtypeStruct((T, SB_WIDTH), BF16),
        jax.ShapeDtypeStruct((T, SB_WIDTH), BF16),
        jax.ShapeDtypeStruct((T, N_BRANCH * D_MODEL), BF16),
        jax.ShapeDtypeStruct((T, GATE_PAD), F32),
    )
    return pl.pallas_call(
        _inproj_kernel,
        grid=(T // tm,),
        in_specs=[row(D_MODEL), _const_spec((1, D_MODEL)), _const_spec((D_MODEL, C_END)),
                  _const_spec((1, GATE_PAD)), _const_spec((1, SB_WIDTH)), _const_spec((1, SB_WIDTH)),
                  _const_spec((SB_WIDTH, SB_WIDTH))],
        out_specs=(row(2 * ML_WIDTH), row(ML_WIDTH), row(ML_WIDTH), row(SB_WIDTH), row(SB_WIDTH),
                   row(SB_WIDTH), row(N_BRANCH * D_MODEL), row(GATE_PAD)),
        out_shape=out_shapes,
        compiler_params=pltpu.CompilerParams(dimension_semantics=("parallel",),
                                             vmem_limit_bytes=VMEM_LIMIT),
        name="inproj",
    )(x2, gmix, w_perm, bg, gq, gk, bd)


def _log_sigmoid(x):
    return jnp.minimum(x, 0.0) - jnp.log1p(jnp.exp(-jnp.abs(x)))


def _mlstm_kernel(qk_ref, vm_ref, om_ref, gcol_ref, grow_ref, cw_ref, cb_ref, tril_ref, triu_ref,
                  out_ref, xbuf, c_ref, m_ref):
    L = ML_CHUNK
    H = ML_HEADS
    Dh = ML_HEAD_DIM
    halo = SUBLANES

    @pl.when(pl.program_id(1) == 0)
    def _():
        xbuf[0:halo, :] = jnp.zeros((halo, 2 * ML_WIDTH), F32)
        c_ref[...] = jnp.zeros_like(c_ref)
        m_ref[...] = jnp.zeros_like(m_ref)

    xbuf[halo:halo + L, :] = qk_ref[...]
    acc = jnp.broadcast_to(cb_ref[...], (L, 2 * ML_WIDTH))
    for j in range(CONV_WIDTH):
        off = halo - (CONV_WIDTH - 1) + j
        acc = acc + cw_ref[j:j + 1, :] * xbuf[off:off + L, :]
    xbuf[0:halo, :] = xbuf[L:L + halo, :]
    qkc = acc * _sigmoid(acc)
    q_all = qkc[:, :ML_WIDTH]
    k_all = qkc[:, ML_WIDTH:] * (Dh ** -0.5)

    gc = gcol_ref[...]
    gr = grow_ref[...]
    lfc = _log_sigmoid(gc)
    lfr = _log_sigmoid(gr)
    bcol = sum(_dot(tril_ref[...], p) for p in _split3(lfc))
    brow = sum(_dot(p, triu_ref[...]) for p in _split3(lfr))

    row_i = lax.broadcasted_iota(jnp.int32, (L, L), 0)
    col_i = lax.broadcasted_iota(jnp.int32, (L, L), 1)
    causal = col_i <= row_i
    ones_aug = jnp.ones((L, Dh), BF16)

    for h in range(H):
        bc = bcol[:, H + h:H + h + 1]
        br = brow[H + h:H + h + 1, :]
        ir = gr[h:h + 1, :]
        ic = gc[:, h:h + 1]
        mprev = m_ref[h:h + 1, 0:1]

        logd = jnp.where(causal, bc - br + ir, -jnp.inf)
        mt = jnp.maximum(bc + mprev, jnp.max(logd, axis=-1, keepdims=True))
        w_intra = jnp.exp(logd - mt)
        w_inter = jnp.exp(bc + mprev - mt)

        sl = slice(h * Dh, (h + 1) * Dh)
        qh = q_all[:, sl].astype(BF16)
        kf = k_all[:, sl]
        kh = kf.astype(BF16)
        vaug = jnp.concatenate([vm_ref[:, sl], ones_aug], axis=1)
        caug = c_ref[h]

        s = _dot_nt(qh, kh) * w_intra
        u = w_inter * _dot(qh, caug.astype(BF16)) + _dot(s.astype(BF16), vaug)
        num = u[:, :Dh]
        den = u[:, Dh:]
        hout = num / jnp.maximum(jnp.abs(den), jnp.exp(-mt))
        out_ref[:, sl] = (om_ref[:, sl].astype(F32) * hout).astype(BF16)

        blast = bc[L - 1:L, :]
        logw_c = blast - bc + ic
        logw_r = blast - br + ir
        mnew = jnp.maximum(blast + mprev, jnp.max(logw_r, axis=-1, keepdims=True))
        w_c = jnp.exp(logw_c - mnew)
        decay = jnp.exp(blast + mprev - mnew)
        kwt = (kf * w_c).T.astype(BF16)
        c_ref[h] = decay * caug + _dot(kwt, vaug)
        m_ref[h:h + 1, :] = jnp.broadcast_to(mnew, (1, LANES))


def _mlstm(qk_pre, vm, om, gcol, grow, conv_w, conv_b, tril, triu):
    B, S, _ = qk_pre.shape
    L = ML_CHUNK
    blk = lambda n: pl.BlockSpec((None, L, n), lambda b, c: (b, c, 0))
    return pl.pallas_call(
        _mlstm_kernel,
        grid=(B, S // L),
        in_specs=[blk(2 * ML_WIDTH), blk(ML_WIDTH), blk(ML_WIDTH), blk(2 * ML_HEADS),
                  pl.BlockSpec((None, 2 * SUBLANES, L), lambda b, c: (b, 0, c)),
                  _const_spec((CONV_WIDTH, 2 * ML_WIDTH)), _const_spec((1, 2 * ML_WIDTH)),
                  _const_spec((L, L)), _const_spec((L, L))],
        out_specs=blk(ML_WIDTH),
        out_shape=jax.ShapeDtypeStruct((B, S, ML_WIDTH), BF16),
        scratch_shapes=[pltpu.VMEM((L + SUBLANES, 2 * ML_WIDTH), F32),
                        pltpu.VMEM((ML_HEADS, ML_HEAD_DIM, 2 * ML_HEAD_DIM), F32),
                        pltpu.VMEM((SUBLANES, LANES), F32)],
        compiler_params=pltpu.CompilerParams(dimension_semantics=("parallel", "arbitrary"),
                                             vmem_limit_bytes=VMEM_LIMIT),
        name="mlstm",
    )(qk_pre, vm, om, gcol, grow, conv_w, conv_b, tril, triu)


def _sb_kernel(q_ref, k_ref, v_ref, tt_ref, out_ref, carry_ref, acc_ref, z_ref, zd_ref, a_ref, qm_ref):
    T = SB_TILE
    G = SB_GROUP
    NG = T // G
    NT = q_ref.shape[0] // T
    lane = lax.broadcasted_iota(jnp.int32, (1, LANES), 1)
    first = lane < SB_HEAD_DIM

    def tile(ref, j):
        return ref[pl.ds(pl.multiple_of(j * T, T), T), :]

    def masked_q(j):
        qp = tile(q_ref, j)
        zero = jnp.zeros_like(qp)
        return jnp.where(first, qp, zero), jnp.where(first, zero, qp)

    def scores(qh, kblk, g):
        return _dot_nt(qh, kblk[g * G:(g + 1) * G, :])

    def weights(hh, r0, r1, g, z, diagonal):
        w = jnp.maximum(z, jnp.log(1.0 + jnp.exp2(jnp.minimum(z, EXP2_MAX))) * LOG2E)
        if diagonal:
            row_i = lax.broadcasted_iota(jnp.int32, (r1 - r0, G), 0)
            col_i = lax.broadcasted_iota(jnp.int32, (r1 - r0, G), 1)
            strict = col_i < row_i
            w = jnp.where(strict, w, 0.0)
        arg = _dot(w.astype(BF16), tt_ref[...]) + z
        rowsum = jnp.sum(w, axis=1, keepdims=True)
        if diagonal:
            carry_ref[hh, r0:r1, :] = jnp.broadcast_to(-rowsum, (r1 - r0, LANES))
            a = jnp.where(strict, jnp.exp2(arg), 0.0)
        else:
            carry = carry_ref[hh, r0:r1, :]
            carry_ref[hh, r0:r1, :] = carry - rowsum
            a = jnp.exp2(arg + jnp.concatenate([carry] * (G // LANES), axis=1))
        a_ref[hh, r0:r1, g * G:(g + 1) * G] = a.astype(BF16)

    def finish(j, last):
        v_last = tile(v_ref, last)
        out = [acc_ref[hh] + _dot(a_ref[hh], v_last) for hh in range(2)]
        res = jnp.where(first, out[0], out[1])
        out_ref[pl.ds(pl.multiple_of(j * T, T), T), :] = res.astype(BF16)
        bits = lax.bitcast_convert_type(res, jnp.uint32)
        return lax.bitcast_convert_type((bits >> 16) >> 16, F32)

    acc_ref[...] = jnp.zeros_like(acc_ref)
    a_ref[...] = jnp.zeros_like(a_ref)
    q0 = masked_q(0)
    k0 = k_ref[0:T, :]
    for hh in range(2):
        for g in range(NG):
            zd_ref[hh, :, g * G:(g + 1) * G] = scores(q0[hh], k0, g)

    def tile_body(qi, last):
        zeros = finish(jnp.maximum(qi - 1, 0), last)
        zeros_q = jnp.concatenate([zeros[0:G]] * (G // LANES), axis=1).astype(BF16)
        for hh in range(2):
            acc_ref[hh] = zeros
        qm = masked_q(qi)
        qn = masked_q(jnp.minimum(qi + 1, NT - 1))
        kd_next = tile(k_ref, jnp.minimum(qi + 1, NT - 1))
        k_next = tile(k_ref, jnp.maximum(qi - 1, 0))
        for hh in range(2):
            qm_ref[hh] = qm[hh]
        for r in range(NG):
            r0, r1 = r * G, (r + 1) * G
            for g in reversed(range(NG)):
                gs = slice(g * G, (g + 1) * G)
                for hh in range(2):
                    if g > r:
                        a_ref[hh, r0:r1, gs] = zeros_q
                        continue
                    weights(hh, r0, r1, g, zd_ref[hh, r0:r1, gs], g == r)
                    zd_ref[hh, r0:r1, gs] = scores(qn[hh][r0:r1], kd_next, g)
        for g in range(NG):
            for hh in range(2):
                z_ref[hh, :, g * G:(g + 1) * G] = scores(qm[hh], k_next, g)

        def block_cond(state):
            i, live = state
            return jnp.logical_and(i < qi, live)

        def block_body(state):
            i, _ = state
            j = qi - 1 - i
            v_prev = tile(v_ref, j + 1)
            k_nxt = tile(k_ref, jnp.maximum(j - 1, 0))
            for g in reversed(range(NG)):
                gs = slice(g * G, (g + 1) * G)
                for hh in range(2):
                    acc_ref[hh] += _dot(a_ref[hh, :, gs], v_prev[gs, :])
                    weights(hh, 0, T, g, z_ref[hh, :, gs], False)
                    z_ref[hh, :, gs] = scores(qm_ref[hh], k_nxt, g)
            return i + 1, jnp.max(carry_ref[...]) >= SB_UNDERFLOW

        n_blocks, _ = lax.while_loop(block_cond, block_body, (jnp.int32(0), jnp.bool_(True)))
        return qi - n_blocks

    last = lax.fori_loop(0, NT, tile_body, jnp.int32(0))
    finish(NT - 1, last)


def _stick_breaking(qs, ks, vs, tt):
    B, S, _ = qs.shape
    T = SB_TILE
    seq = pl.BlockSpec((None, S, LANES), lambda b, p: (b, 0, p))
    return pl.pallas_call(
        _sb_kernel,
        grid=(B, SB_WIDTH // LANES),
        in_specs=[seq, seq, seq, _const_spec((SB_GROUP, SB_GROUP))],
        out_specs=seq,
        out_shape=jax.ShapeDtypeStruct((B, S, SB_WIDTH), BF16),
        scratch_shapes=[pltpu.VMEM((2, T, LANES), F32), pltpu.VMEM((2, T, LANES), F32),
                        pltpu.VMEM((2, T, T), F32), pltpu.VMEM((2, T, T), F32),
                        pltpu.VMEM((2, T, T), BF16), pltpu.VMEM((2, T, LANES), BF16)],
        compiler_params=pltpu.CompilerParams(
            dimension_semantics=("parallel", "parallel"),
            vmem_limit_bytes=VMEM_LIMIT),
        name="stickbreak",
    )(qs, ks, vs, tt)


def _mix_ffn_kernel(x_ref, ya_ref, yb_ref, g_ref, wa_ref, wb_ref, wo_ref, gffn_ref, wgu_ref, wd_ref,
                    out_ref):
    pa = _dot(ya_ref[...], wa_ref[...])
    pb = _dot(yb_ref[...], wb_ref[...])
    mix = g_ref[:, :D_MODEL].astype(F32) * pa + g_ref[:, D_MODEL:].astype(F32) * pb
    xn = x_ref[...] + _dot(mix.astype(BF16), wo_ref[...])
    ms = jnp.mean(xn * xn, axis=-1, keepdims=True)
    h2 = (xn * lax.rsqrt(ms + EPS) * gffn_ref[...]).astype(BF16)
    acc = xn
    for c in range(D_FF // FF_CHUNK):
        lo = c * FF_CHUNK
        gt = _dot(h2, wgu_ref[:, lo:lo + FF_CHUNK])
        up = _dot(h2, wgu_ref[:, D_FF + lo:D_FF + lo + FF_CHUNK])
        act = (gt * _sigmoid(gt) * up).astype(BF16)
        acc = acc + _dot(act, wd_ref[lo:lo + FF_CHUNK, :])
    out_ref[...] = acc


def _mix_ffn(x2, ya, yb, g, wa, wb, wo, gffn, wgu, wd):
    T = x2.shape[0]
    tm = ROW_TILE
    row = lambda n: pl.BlockSpec((tm, n), lambda i: (i, 0))
    return pl.pallas_call(
        _mix_ffn_kernel,
        grid=(T // tm,),
        in_specs=[row(D_MODEL), row(ML_WIDTH), row(SB_WIDTH), row(N_BRANCH * D_MODEL),
                  _const_spec((ML_WIDTH, D_MODEL)), _const_spec((SB_WIDTH, D_MODEL)),
                  _const_spec((D_MODEL, D_MODEL)), _const_spec((1, D_MODEL)),
                  _const_spec((D_MODEL, 2 * D_FF)), _const_spec((D_FF, D_MODEL))],
        out_specs=row(D_MODEL),
        out_shape=jax.ShapeDtypeStruct((T, D_MODEL), F32),
        compiler_params=pltpu.CompilerParams(dimension_semantics=("parallel",),
                                             vmem_limit_bytes=VMEM_LIMIT),
        name="mixffn",
    )(x2, ya, yb, g, wa, wb, wo, gffn, wgu, wd)


def _constants():
    L = ML_CHUNK
    r = np.arange(L)
    tril = (r[None, :] <= r[:, None]).astype(np.float32)
    g = np.arange(SB_WIDTH) // SB_HEAD_DIM
    bd = (g[:, None] == g[None, :]).astype(np.float32)
    s = np.arange(SB_GROUP)
    tt = -(s[:, None] >= s[None, :]).astype(np.float32)
    return (jnp.asarray(tril, BF16), jnp.asarray(tril.T, BF16), jnp.asarray(bd, BF16),
            jnp.asarray(tt, BF16))


def kernel(x, g_mix, w_in, conv_w, conv_b, b_gates, g_q, g_k, w_br_a, w_br_b, w_out, g_ffn, w_gu,
           w_down):
    B, S, D = x.shape
    depth = g_mix.shape[0]
    T = B * S
    tril, triu, bd, tt = _constants()
    n_gate = 2 * ML_HEADS
    o_gates = 2 * ML_WIDTH + 2 * ML_WIDTH

    x2 = x.reshape(T, D)
    for l in range(depth):
        w = w_in[l]
        w_perm = jnp.concatenate(
            [w[:, :o_gates], w[:, o_gates + n_gate:],
             jnp.pad(w[:, o_gates:o_gates + n_gate], ((0, 0), (0, GATE_PAD - n_gate)))],
            axis=1).astype(BF16)
        bg = jnp.pad(b_gates[l], (0, GATE_PAD - n_gate)).reshape(1, GATE_PAD)
        gq = jnp.tile(g_q[l], SB_HEADS).reshape(1, SB_WIDTH)
        gk = jnp.tile(g_k[l], SB_HEADS).reshape(1, SB_WIDTH)

        qk_pre, vm, om, qs, ks, vs, gp, gates = _inproj(
            x2, g_mix[l].reshape(1, D), w_perm, bg, gq, gk, bd)

        gcol = gates[:, :n_gate].reshape(B, S, n_gate)
        grow = jnp.pad(jnp.swapaxes(gcol, 1, 2), ((0, 0), (0, 2 * SUBLANES - n_gate), (0, 0)))
        ya = _mlstm(qk_pre.reshape(B, S, -1), vm.reshape(B, S, -1), om.reshape(B, S, -1),
                    gcol, grow, conv_w[l], conv_b[l].reshape(1, -1), tril, triu)
        yb = _stick_breaking(qs.reshape(B, S, -1), ks.reshape(B, S, -1), vs.reshape(B, S, -1), tt)

        x2 = _mix_ffn(x2, ya.reshape(T, -1), yb.reshape(T, -1), gp,
                      w_br_a[l].astype(BF16), w_br_b[l].astype(BF16), w_out[l].astype(BF16),
                      g_ffn[l].reshape(1, D), w_gu[l].astype(BF16), w_down[l].astype(BF16))
    return x2.reshape(B, S, D)
```

```python
import jax
import jax.numpy as jnp
import numpy as np
from jax import lax
from jax.experimental import pallas as pl
from jax.experimental.pallas import tpu as pltpu

D_MODEL = 1024
ML_HEADS = 4
ML_HEAD_DIM = 128
ML_WIDTH = ML_HEADS * ML_HEAD_DIM
CONV_WIDTH = 4
SB_HEADS = 8
SB_HEAD_DIM = 64
SB_WIDTH = SB_HEADS * SB_HEAD_DIM
N_BRANCH = 2
D_FF = 2816
EPS = 1e-6

LANES = 128
SUBLANES = 8
GATE_PAD = LANES
ROW_TILE = 512
ML_CHUNK = 256
SB_TILE = 256
SB_GROUP = 256
SB_TILES_PER_STEP = 1
LOG2E = 1.4426950408889634
EXP2_MAX = 126.0
SB_UNDERFLOW = -160.0
SB_NO_BLOCK = -1.0e4
FF_CHUNK = 256
VMEM_LIMIT = 56 * 1024 * 1024

F32 = jnp.float32
BF16 = jnp.bfloat16

C_QK = 0
C_VM = C_QK + 2 * ML_WIDTH
C_OM = C_VM + ML_WIDTH
C_QS = C_OM + ML_WIDTH
C_KS = C_QS + SB_WIDTH
C_VS = C_KS + SB_WIDTH
C_GP = C_VS + SB_WIDTH
C_GATES = C_GP + N_BRANCH * D_MODEL
C_END = C_GATES + GATE_PAD


def _const_spec(shape):
    nd = len(shape)
    return pl.BlockSpec(shape, lambda *_: (0,) * nd, pipeline_mode=pl.Buffered(1))


def _split3(a):
    hi = a.astype(BF16)
    r = a - hi.astype(F32)
    mid = r.astype(BF16)
    lo = (r - mid.astype(F32)).astype(BF16)
    return hi, mid, lo


def _dot(a, b):
    return jnp.dot(a, b, preferred_element_type=F32)


def _dot_nt(a, b):
    return lax.dot_general(a, b, (((1,), (1,)), ((), ())), preferred_element_type=F32)


def _sigmoid(x):
    return 0.5 * jnp.tanh(0.5 * x) + 0.5


def _inproj_kernel(x_ref, gmix_ref, w_ref, bg_ref, gq_ref, gk_ref, bd_ref,
                   qk_ref, vm_ref, om_ref, qs_ref, ks_ref, vs_ref, gp_ref, gates_ref):
    x = x_ref[...]
    ms = jnp.mean(x * x, axis=-1, keepdims=True)
    h = (x * lax.rsqrt(ms + EPS) * gmix_ref[...]).astype(BF16)

    def proj(a, b):
        return _dot(h, w_ref[:, a:b])

    def head_rms(y, g):
        gs = _dot((y * y).astype(BF16), bd_ref[...])
        return y * lax.rsqrt(gs * (1.0 / SB_HEAD_DIM) + EPS) * g

    qk_ref[...] = proj(C_QK, C_VM)
    vm_ref[...] = proj(C_VM, C_OM).astype(BF16)
    om_ref[...] = _sigmoid(proj(C_OM, C_QS)).astype(BF16)
    qs_ref[...] = (head_rms(proj(C_QS, C_KS), gq_ref[...]) * (SB_HEAD_DIM ** -0.5 * LOG2E)).astype(BF16)
    ks_ref[...] = head_rms(proj(C_KS, C_VS), gk_ref[...]).astype(BF16)
    vs_ref[...] = proj(C_VS, C_GP).astype(BF16)
    gp_ref[...] = _sigmoid(proj(C_GP, C_GATES)).astype(BF16)
    gates_ref[...] = (proj(C_GATES, C_END) + bg_ref[...])[:, :2 * ML_HEADS]


def _inproj(x2, gmix, w_perm, bg, gq, gk, bd):
    T = x2.shape[0]
    tm = ROW_TILE
    row = lambda n: pl.BlockSpec((tm, n), lambda i: (i, 0))
    out_shapes = (
        jax.ShapeDtypeStruct((T, 2 * ML_WIDTH), F32),
        jax.ShapeDtypeStruct((T, ML_WIDTH), BF16),
        jax.ShapeDtypeStruct((T, ML_WIDTH), BF16),
        jax.ShapeDtypeStruct((T, SB_WIDTH), BF16),
        jax.ShapeDtypeStruct((T, SB_WIDTH), BF16),
        jax.ShapeDtypeStruct((T, SB_WIDTH), BF16),
        jax.ShapeDtypeStruct((T, N_BRANCH * D_MODEL), BF16),
        jax.ShapeDtypeStruct((T, 2 * ML_HEADS), F32),
    )
    return pl.pallas_call(
        _inproj_kernel,
        grid=(T // tm,),
        in_specs=[row(D_MODEL), _const_spec((1, D_MODEL)), _const_spec((D_MODEL, C_END)),
                  _const_spec((1, GATE_PAD)), _const_spec((1, SB_WIDTH)), _const_spec((1, SB_WIDTH)),
                  _const_spec((SB_WIDTH, SB_WIDTH))],
        out_specs=(row(2 * ML_WIDTH), row(ML_WIDTH), row(ML_WIDTH), row(SB_WIDTH), row(SB_WIDTH),
                   row(SB_WIDTH), row(N_BRANCH * D_MODEL), row(2 * ML_HEADS)),
        out_shape=out_shapes,
        compiler_params=pltpu.CompilerParams(dimension_semantics=("parallel",),
                                             vmem_limit_bytes=VMEM_LIMIT),
        name="inproj",
    )(x2, gmix, w_perm, bg, gq, gk, bd)


def _log_sigmoid(x):
    return jnp.minimum(x, 0.0) - jnp.log1p(jnp.exp(-jnp.abs(x)))


def _mlstm_kernel(qk_ref, vm_ref, om_ref, gcol_ref, grow_ref, cw_ref, cb_ref, tril_ref, triu_ref,
                  out_ref, xbuf, c_ref, m_ref):
    L = ML_CHUNK
    H = ML_HEADS
    Dh = ML_HEAD_DIM
    halo = SUBLANES

    @pl.when(pl.program_id(1) == 0)
    def _():
        xbuf[0:halo, :] = jnp.zeros((halo, 2 * ML_WIDTH), F32)
        c_ref[...] = jnp.zeros_like(c_ref)
        m_ref[...] = jnp.zeros_like(m_ref)

    xbuf[halo:halo + L, :] = qk_ref[...]
    acc = jnp.broadcast_to(cb_ref[...], (L, 2 * ML_WIDTH))
    for j in range(CONV_WIDTH):
        off = halo - (CONV_WIDTH - 1) + j
        acc = acc + cw_ref[j:j + 1, :] * xbuf[off:off + L, :]
    xbuf[0:halo, :] = xbuf[L:L + halo, :]
    qkc = acc * _sigmoid(acc)
    q_all = qkc[:, :ML_WIDTH]
    k_all = qkc[:, ML_WIDTH:] * (Dh ** -0.5)

    gc = gcol_ref[...]
    gr = grow_ref[...]
    lfc = _log_sigmoid(gc)
    lfr = _log_sigmoid(gr)
    bcol = sum(_dot(tril_ref[...], p) for p in _split3(lfc))
    brow = sum(_dot(p, triu_ref[...]) for p in _split3(lfr))

    row_i = lax.broadcasted_iota(jnp.int32, (L, L), 0)
    col_i = lax.broadcasted_iota(jnp.int32, (L, L), 1)
    causal = col_i <= row_i
    ones_aug = jnp.ones((L, Dh), BF16)

    for h in range(H):
        bc = bcol[:, H + h:H + h + 1]
        br = brow[H + h:H + h + 1, :]
        ir = gr[h:h + 1, :]
        ic = gc[:, h:h + 1]
        mprev = m_ref[h:h + 1, 0:1]

        logd = jnp.where(causal, bc - br + ir, -jnp.inf)
        mt = jnp.maximum(bc + mprev, jnp.max(logd, axis=-1, keepdims=True))
        w_intra = jnp.exp(logd - mt)
        w_inter = jnp.exp(bc + mprev - mt)

        sl = slice(h * Dh, (h + 1) * Dh)
        qh = q_all[:, sl].astype(BF16)
        kf = k_all[:, sl]
        kh = kf.astype(BF16)
        vaug = jnp.concatenate([vm_ref[:, sl], ones_aug], axis=1)
        caug = c_ref[h]

        s = _dot_nt(qh, kh) * w_intra
        u = w_inter * _dot(qh, caug.astype(BF16)) + _dot(s.astype(BF16), vaug)
        num = u[:, :Dh]
        den = u[:, Dh:]
        hout = num / jnp.maximum(jnp.abs(den), jnp.exp(-mt))
        out_ref[:, sl] = (om_ref[:, sl].astype(F32) * hout).astype(BF16)

        blast = bc[L - 1:L, :]
        logw_c = blast - bc + ic
        logw_r = blast - br + ir
        mnew = jnp.maximum(blast + mprev, jnp.max(logw_r, axis=-1, keepdims=True))
        w_c = jnp.exp(logw_c - mnew)
        decay = jnp.exp(blast + mprev - mnew)
        kwt = (kf * w_c).T.astype(BF16)
        c_ref[h] = decay * caug + _dot(kwt, vaug)
        m_ref[h:h + 1, :] = jnp.broadcast_to(mnew, (1, LANES))


def _mlstm(qk_pre, vm, om, gcol, grow, conv_w, conv_b, tril, triu):
    B, S, _ = qk_pre.shape
    L = ML_CHUNK
    blk = lambda n: pl.BlockSpec((None, L, n), lambda b, c: (b, c, 0))
    return pl.pallas_call(
        _mlstm_kernel,
        grid=(B, S // L),
        in_specs=[blk(2 * ML_WIDTH), blk(ML_WIDTH), blk(ML_WIDTH), blk(2 * ML_HEADS),
                  pl.BlockSpec((None, 2 * SUBLANES, L), lambda b, c: (b, 0, c)),
                  _const_spec((CONV_WIDTH, 2 * ML_WIDTH)), _const_spec((1, 2 * ML_WIDTH)),
                  _const_spec((L, L)), _const_spec((L, L))],
        out_specs=blk(ML_WIDTH),
        out_shape=jax.ShapeDtypeStruct((B, S, ML_WIDTH), BF16),
        scratch_shapes=[pltpu.VMEM((L + SUBLANES, 2 * ML_WIDTH), F32),
                        pltpu.VMEM((ML_HEADS, ML_HEAD_DIM, 2 * ML_HEAD_DIM), F32),
                        pltpu.VMEM((SUBLANES, LANES), F32)],
        compiler_params=pltpu.CompilerParams(dimension_semantics=("parallel", "arbitrary"),
                                             vmem_limit_bytes=VMEM_LIMIT),
        name="mlstm",
    )(qk_pre, vm, om, gcol, grow, conv_w, conv_b, tril, triu)


def _sb_kernel(q_ref, k_ref, v_ref, tt_ref, out_ref, carry_ref, acc_ref, zd_ref, zn_ref, a_ref, qm_ref):
    T = SB_TILE
    P = SB_TILES_PER_STEP
    assert T == SB_GROUP
    NT = q_ref.shape[0] // T
    assert NT % P == 0
    lane = lax.broadcasted_iota(jnp.int32, (1, LANES), 1)
    first = lane < SB_HEAD_DIM

    def tile(ref, j):
        return ref[pl.ds(pl.multiple_of(j * T, T), T), :]

    def masked_q(j):
        qp = tile(q_ref, j)
        zero = jnp.zeros_like(qp)
        return jnp.where(first, qp, zero), jnp.where(first, zero, qp)

    def weights(t, hh, z, diagonal):
        w = jnp.maximum(z, jnp.log(1.0 + jnp.exp2(jnp.minimum(z, EXP2_MAX))) * LOG2E)
        if diagonal:
            row_i = lax.broadcasted_iota(jnp.int32, (T, T), 0)
            col_i = lax.broadcasted_iota(jnp.int32, (T, T), 1)
            strict = col_i < row_i
            w = jnp.where(strict, w, 0.0)
        arg = _dot(w.astype(BF16), tt_ref[...]) + z
        rowsum = jnp.sum(w, axis=1, keepdims=True)
        if diagonal:
            carry_ref[t, hh] = jnp.broadcast_to(-rowsum, (T, LANES))
            return jnp.where(strict, jnp.exp2(arg), 0.0).astype(BF16)
        carry = carry_ref[t, hh]
        carry_ref[t, hh] = carry - rowsum
        return jnp.exp2(arg + jnp.concatenate([carry] * (T // LANES), axis=1)).astype(BF16)

    def finish(t, j):
        v_diag = tile(v_ref, j)
        v_left = tile(v_ref, jnp.maximum(j - 1, 0))
        out = [acc_ref[t, hh] + _dot(a_ref[t, hh, 0], v_diag) + _dot(a_ref[t, hh, 1], v_left)
               for hh in range(2)]
        res = jnp.where(first, out[0], out[1])
        out_ref[pl.ds(pl.multiple_of(j * T, T), T), :] = res.astype(BF16)
        bits = lax.bitcast_convert_type(res, jnp.uint32)
        return lax.bitcast_convert_type((bits >> 16) >> 16, F32)

    def operands(j):
        j = jnp.minimum(j, NT - 1)
        return masked_q(j), tile(k_ref, j), tile(k_ref, jnp.maximum(j - 1, 0))

    acc_ref[...] = jnp.zeros_like(acc_ref)
    a_ref[...] = jnp.zeros_like(a_ref)
    for t in range(P):
        qn, k_diag, k_left = operands(t)
        for hh in range(2):
            zd_ref[t, hh] = _dot_nt(qn[hh], k_diag)
            zn_ref[t, hh] = _dot_nt(qn[hh], k_left) if t > 0 else jnp.full((T, T), SB_NO_BLOCK, F32)

    def step_body(s, c):
        prev = jnp.maximum(s - 1, 0) * P
        for t in range(P):
            zeros = finish(t, prev + t)
            for hh in range(2):
                acc_ref[t, hh] = zeros
        for t in range(P):
            qm = masked_q(s * P + t)
            for hh in range(2):
                qm_ref[t, hh] = qm[hh]
        nxt = [operands((s + 1) * P + t) for t in range(P)]
        for t in range(P):
            qn, k_diag, _ = nxt[t]
            for hh in range(2):
                a_ref[t, hh, 0] = weights(t, hh, zd_ref[t, hh], True)
                zd_ref[t, hh] = _dot_nt(qn[hh], k_diag)
        for t in range(P):
            qn, _, k_left = nxt[t]
            for hh in range(2):
                a_ref[t, hh, 1] = weights(t, hh, zn_ref[t, hh], False)
                zn_ref[t, hh] = _dot_nt(qn[hh], k_left)

        for t in range(P):
            def live(t=t):
                return jnp.max(carry_ref[t]) >= SB_UNDERFLOW

            def block_cond(state):
                j, more = state
                return jnp.logical_and(j >= 0, more)

            def block_body(state, t=t, live=live):
                j, _ = state
                k_blk = tile(k_ref, j)
                v_blk = tile(v_ref, j)
                for hh in range(2):
                    a = weights(t, hh, _dot_nt(qm_ref[t, hh], k_blk), False)
                    acc_ref[t, hh] += _dot(a, v_blk)
                return j - 1, live()

            lax.while_loop(block_cond, block_body, (s * P + t - 2, live()))
        return c

    lax.fori_loop(0, NT // P, step_body, 0)
    for t in range(P):
        finish(t, NT - P + t)


def _stick_breaking(qs, ks, vs, tt):
    B, S, _ = qs.shape
    T = SB_TILE
    P = SB_TILES_PER_STEP
    seq = pl.BlockSpec((None, S, LANES), lambda b, p: (b, 0, p))
    return pl.pallas_call(
        _sb_kernel,
        grid=(B, SB_WIDTH // LANES),
        in_specs=[seq, seq, seq, _const_spec((SB_GROUP, SB_GROUP))],
        out_specs=seq,
        out_shape=jax.ShapeDtypeStruct((B, S, SB_WIDTH), BF16),
        scratch_shapes=[pltpu.VMEM((P, 2, T, LANES), F32), pltpu.VMEM((P, 2, T, LANES), F32),
                        pltpu.VMEM((P, 2, T, T), F32), pltpu.VMEM((P, 2, T, T), F32),
                        pltpu.VMEM((P, 2, 2, T, T), BF16), pltpu.VMEM((P, 2, T, LANES), BF16)],
        compiler_params=pltpu.CompilerParams(
            dimension_semantics=("parallel", "parallel"),
            vmem_limit_bytes=VMEM_LIMIT),
        name="stickbreak",
    )(qs, ks, vs, tt)


def _mix_ffn_kernel(x_ref, ya_ref, yb_ref, g_ref, wa_ref, wb_ref, wo_ref, gffn_ref, wgu_ref, wd_ref,
                    out_ref):
    pa = _dot(ya_ref[...], wa_ref[...])
    pb = _dot(yb_ref[...], wb_ref[...])
    mix = g_ref[:, :D_MODEL].astype(F32) * pa + g_ref[:, D_MODEL:].astype(F32) * pb
    xn = x_ref[...] + _dot(mix.astype(BF16), wo_ref[...])
    ms = jnp.mean(xn * xn, axis=-1, keepdims=True)
    h2 = (xn * lax.rsqrt(ms + EPS) * gffn_ref[...]).astype(BF16)
    acc = xn
    for c in range(D_FF // FF_CHUNK):
        lo = c * FF_CHUNK
        gt = _dot(h2, wgu_ref[:, lo:lo + FF_CHUNK])
        up = _dot(h2, wgu_ref[:, D_FF + lo:D_FF + lo + FF_CHUNK])
        act = (gt * _sigmoid(gt) * up).astype(BF16)
        acc = acc + _dot(act, wd_ref[lo:lo + FF_CHUNK, :])
    out_ref[...] = acc


def _mix_ffn(x2, ya, yb, g, wa, wb, wo, gffn, wgu, wd):
    T = x2.shape[0]
    tm = ROW_TILE
    row = lambda n: pl.BlockSpec((tm, n), lambda i: (i, 0))
    return pl.pallas_call(
        _mix_ffn_kernel,
        grid=(T // tm,),
        in_specs=[row(D_MODEL), row(ML_WIDTH), row(SB_WIDTH), row(N_BRANCH * D_MODEL),
                  _const_spec((ML_WIDTH, D_MODEL)), _const_spec((SB_WIDTH, D_MODEL)),
                  _const_spec((D_MODEL, D_MODEL)), _const_spec((1, D_MODEL)),
                  _const_spec((D_MODEL, 2 * D_FF)), _const_spec((D_FF, D_MODEL))],
        out_specs=row(D_MODEL),
        out_shape=jax.ShapeDtypeStruct((T, D_MODEL), F32),
        compiler_params=pltpu.CompilerParams(dimension_semantics=("parallel",),
                                             vmem_limit_bytes=VMEM_LIMIT),
        name="mixffn",
    )(x2, ya, yb, g, wa, wb, wo, gffn, wgu, wd)


def _constants():
    L = ML_CHUNK
    r = np.arange(L)
    tril = (r[None, :] <= r[:, None]).astype(np.float32)
    g = np.arange(SB_WIDTH) // SB_HEAD_DIM
    bd = (g[:, None] == g[None, :]).astype(np.float32)
    s = np.arange(SB_GROUP)
    tt = -(s[:, None] >= s[None, :]).astype(np.float32)
    return (jnp.asarray(tril, BF16), jnp.asarray(tril.T, BF16), jnp.asarray(bd, BF16),
            jnp.asarray(tt, BF16))


def kernel(x, g_mix, w_in, conv_w, conv_b, b_gates, g_q, g_k, w_br_a, w_br_b, w_out, g_ffn, w_gu,
           w_down):
    B, S, D = x.shape
    depth = g_mix.shape[0]
    T = B * S
    tril, triu, bd, tt = _constants()
    n_gate = 2 * ML_HEADS
    o_gates = 2 * ML_WIDTH + 2 * ML_WIDTH

    x2 = x.reshape(T, D)
    for l in range(depth):
        w = w_in[l]
        w_perm = jnp.concatenate(
            [w[:, :o_gates].astype(BF16), w[:, o_gates + n_gate:].astype(BF16),
             jnp.pad(w[:, o_gates:o_gates + n_gate].astype(BF16), ((0, 0), (0, GATE_PAD - n_gate)))],
            axis=1)
        bg = jnp.pad(b_gates[l], (0, GATE_PAD - n_gate)).reshape(1, GATE_PAD)
        gq = jnp.tile(g_q[l], SB_HEADS).reshape(1, SB_WIDTH)
        gk = jnp.tile(g_k[l], SB_HEADS).reshape(1, SB_WIDTH)

        qk_pre, vm, om, qs, ks, vs, gp, gates = _inproj(
            x2, g_mix[l].reshape(1, D), w_perm, bg, gq, gk, bd)

        gcol = gates.reshape(B, S, n_gate)
        grow = jnp.pad(jnp.swapaxes(gcol, 1, 2), ((0, 0), (0, 2 * SUBLANES - n_gate), (0, 0)))
        ya = _mlstm(qk_pre.reshape(B, S, -1), vm.reshape(B, S, -1), om.reshape(B, S, -1),
                    gcol, grow, conv_w[l], conv_b[l].reshape(1, -1), tril, triu)
        yb = _stick_breaking(qs.reshape(B, S, -1), ks.reshape(B, S, -1), vs.reshape(B, S, -1), tt)

        x2 = _mix_ffn(x2, ya.reshape(T, -1), yb.reshape(T, -1), gp,
                      w_br_a[l].astype(BF16), w_br_b[l].astype(BF16), w_out[l].astype(BF16),
                      g_ffn[l].reshape(1, D), w_gu[l].astype(BF16), w_down[l].astype(BF16))
    return x2.reshape(B, S, D)
```

```python
import jax
import jax.numpy as jnp
import numpy as np
from jax import lax
from jax.experimental import pallas as pl
from jax.experimental.pallas import tpu as pltpu

D_MODEL = 1024
ML_HEADS = 4
ML_HEAD_DIM = 128
ML_WIDTH = ML_HEADS * ML_HEAD_DIM
CONV_WIDTH = 4
SB_HEADS = 8
SB_HEAD_DIM = 64
SB_WIDTH = SB_HEADS * SB_HEAD_DIM
N_BRANCH = 2
D_FF = 2816
EPS = 1e-6

LANES = 128
SUBLANES = 8
GATE_PAD = LANES
ROW_TILE = 512
ML_CHUNK = 256
SB_TILE = 256
SB_GROUP = 256
SB_TILES_PER_STEP = 1
LOG2E = 1.4426950408889634
EXP2_MAX = 126.0
SB_UNDERFLOW = -160.0
SB_NO_BLOCK = -1.0e4
FF_CHUNK = 256
VMEM_LIMIT = 56 * 1024 * 1024

F32 = jnp.float32
BF16 = jnp.bfloat16

C_QK = 0
C_VM = C_QK + 2 * ML_WIDTH
C_OM = C_VM + ML_WIDTH
C_QS = C_OM + ML_WIDTH
C_KS = C_QS + SB_WIDTH
C_VS = C_KS + SB_WIDTH
C_GP = C_VS + SB_WIDTH
C_GATES = C_GP + N_BRANCH * D_MODEL
C_END = C_GATES + GATE_PAD


def _const_spec(shape):
    nd = len(shape)
    return pl.BlockSpec(shape, lambda *_: (0,) * nd, pipeline_mode=pl.Buffered(1))


def _layer_spec(shape, layer):
    nd = len(shape)
    return pl.BlockSpec((None,) + tuple(shape), lambda *_: (layer,) + (0,) * nd,
                        pipeline_mode=pl.Buffered(1))


def _split3(a):
    hi = a.astype(BF16)
    r = a - hi.astype(F32)
    mid = r.astype(BF16)
    lo = (r - mid.astype(F32)).astype(BF16)
    return hi, mid, lo


def _dot(a, b):
    return jnp.dot(a, b, preferred_element_type=F32)


def _dot_nt(a, b):
    return lax.dot_general(a, b, (((1,), (1,)), ((), ())), preferred_element_type=F32)


def _sigmoid(x):
    return 0.5 * jnp.tanh(0.5 * x) + 0.5


def _inproj_kernel(x_ref, gmix_ref, w_ref, bg_ref, gq_ref, gk_ref, bd_ref,
                   qk_ref, vm_ref, om_ref, qs_ref, ks_ref, vs_ref, gp_ref, gates_ref):
    x = x_ref[...]
    ms = jnp.mean(x * x, axis=-1, keepdims=True)
    h = (x * lax.rsqrt(ms + EPS) * gmix_ref[...]).astype(BF16)

    def proj(a, b):
        return _dot(h, w_ref[:, a:b])

    def head_rms(y, g):
        gs = _dot((y * y).astype(BF16), bd_ref[...])
        return y * lax.rsqrt(gs * (1.0 / SB_HEAD_DIM) + EPS) * g

    qk_ref[...] = proj(C_QK, C_VM)
    vm_ref[...] = proj(C_VM, C_OM).astype(BF16)
    om_ref[...] = _sigmoid(proj(C_OM, C_QS)).astype(BF16)
    qs_ref[...] = (head_rms(proj(C_QS, C_KS), gq_ref[...]) * (SB_HEAD_DIM ** -0.5 * LOG2E)).astype(BF16)
    ks_ref[...] = head_rms(proj(C_KS, C_VS), gk_ref[...]).astype(BF16)
    vs_ref[...] = proj(C_VS, C_GP).astype(BF16)
    gp_ref[...] = _sigmoid(proj(C_GP, C_GATES)).astype(BF16)
    gates_ref[...] = (proj(C_GATES, C_END) + bg_ref[...])[:, :2 * ML_HEADS]


def _inproj(x2, layer, gmix, w_perm, bg, gq, gk, bd):
    T = x2.shape[0]
    tm = ROW_TILE
    row = lambda n: pl.BlockSpec((tm, n), lambda i: (i, 0))
    out_shapes = (
        jax.ShapeDtypeStruct((T, 2 * ML_WIDTH), F32),
        jax.ShapeDtypeStruct((T, ML_WIDTH), BF16),
        jax.ShapeDtypeStruct((T, ML_WIDTH), BF16),
        jax.ShapeDtypeStruct((T, SB_WIDTH), BF16),
        jax.ShapeDtypeStruct((T, SB_WIDTH), BF16),
        jax.ShapeDtypeStruct((T, SB_WIDTH), BF16),
        jax.ShapeDtypeStruct((T, N_BRANCH * D_MODEL), BF16),
        jax.ShapeDtypeStruct((T, 2 * ML_HEADS), F32),
    )
    return pl.pallas_call(
        _inproj_kernel,
        grid=(T // tm,),
        in_specs=[row(D_MODEL), _const_spec((1, D_MODEL)), _layer_spec((D_MODEL, C_END), layer),
                  _const_spec((1, GATE_PAD)), _const_spec((1, SB_WIDTH)), _const_spec((1, SB_WIDTH)),
                  _const_spec((SB_WIDTH, SB_WIDTH))],
        out_specs=(row(2 * ML_WIDTH), row(ML_WIDTH), row(ML_WIDTH), row(SB_WIDTH), row(SB_WIDTH),
                   row(SB_WIDTH), row(N_BRANCH * D_MODEL), row(2 * ML_HEADS)),
        out_shape=out_shapes,
        compiler_params=pltpu.CompilerParams(dimension_semantics=("parallel",),
                                             vmem_limit_bytes=VMEM_LIMIT),
        name="inproj",
    )(x2, gmix, w_perm, bg, gq, gk, bd)


def _log_sigmoid(x):
    return jnp.minimum(x, 0.0) - jnp.log1p(jnp.exp(-jnp.abs(x)))


def _mlstm_kernel(qk_ref, vm_ref, om_ref, gcol_ref, grow_ref, cw_ref, cb_ref, tril_ref, triu_ref,
                  out_ref, xbuf, c_ref, m_ref):
    L = ML_CHUNK
    H = ML_HEADS
    Dh = ML_HEAD_DIM
    halo = SUBLANES

    @pl.when(pl.program_id(1) == 0)
    def _():
        xbuf[0:halo, :] = jnp.zeros((halo, 2 * ML_WIDTH), F32)
        c_ref[...] = jnp.zeros_like(c_ref)
        m_ref[...] = jnp.zeros_like(m_ref)

    xbuf[halo:halo + L, :] = qk_ref[...]
    acc = jnp.broadcast_to(cb_ref[...], (L, 2 * ML_WIDTH))
    for j in range(CONV_WIDTH):
        off = halo - (CONV_WIDTH - 1) + j
        acc = acc + cw_ref[j:j + 1, :] * xbuf[off:off + L, :]
    xbuf[0:halo, :] = xbuf[L:L + halo, :]
    qkc = acc * _sigmoid(acc)
    q_all = qkc[:, :ML_WIDTH]
    k_all = qkc[:, ML_WIDTH:] * (Dh ** -0.5)

    gc = gcol_ref[...]
    gr = grow_ref[...]
    lfc = _log_sigmoid(gc)
    lfr = _log_sigmoid(gr)
    bcol = sum(_dot(tril_ref[...], p) for p in _split3(lfc))
    brow = sum(_dot(p, triu_ref[...]) for p in _split3(lfr))

    row_i = lax.broadcasted_iota(jnp.int32, (L, L), 0)
    col_i = lax.broadcasted_iota(jnp.int32, (L, L), 1)
    causal = col_i <= row_i
    ones_aug = jnp.ones((L, Dh), BF16)

    for h in range(H):
        bc = bcol[:, H + h:H + h + 1]
        br = brow[H + h:H + h + 1, :]
        ir = gr[h:h + 1, :]
        ic = gc[:, h:h + 1]
        mprev = m_ref[h:h + 1, 0:1]

        logd = jnp.where(causal, bc - br + ir, -jnp.inf)
        mt = jnp.maximum(bc + mprev, jnp.max(logd, axis=-1, keepdims=True))
        w_intra = jnp.exp(logd - mt)
        w_inter = jnp.exp(bc + mprev - mt)

        sl = slice(h * Dh, (h + 1) * Dh)
        qh = q_all[:, sl].astype(BF16)
        kf = k_all[:, sl]
        kh = kf.astype(BF16)
        vaug = jnp.concatenate([vm_ref[:, sl], ones_aug], axis=1)
        caug = c_ref[h]

        s = _dot_nt(qh, kh) * w_intra
        u = w_inter * _dot(qh, caug.astype(BF16)) + _dot(s.astype(BF16), vaug)
        num = u[:, :Dh]
        den = u[:, Dh:]
        hout = num / jnp.maximum(jnp.abs(den), jnp.exp(-mt))
        out_ref[:, sl] = (om_ref[:, sl].astype(F32) * hout).astype(BF16)

        blast = bc[L - 1:L, :]
        logw_c = blast - bc + ic
        logw_r = blast - br + ir
        mnew = jnp.maximum(blast + mprev, jnp.max(logw_r, axis=-1, keepdims=True))
        w_c = jnp.exp(logw_c - mnew)
        decay = jnp.exp(blast + mprev - mnew)
        kwt = (kf * w_c).T.astype(BF16)
        c_ref[h] = decay * caug + _dot(kwt, vaug)
        m_ref[h:h + 1, :] = jnp.broadcast_to(mnew, (1, LANES))


def _mlstm(qk_pre, vm, om, gcol, grow, conv_w, conv_b, tril, triu):
    B, S, _ = qk_pre.shape
    L = ML_CHUNK
    blk = lambda n: pl.BlockSpec((None, L, n), lambda b, c: (b, c, 0))
    return pl.pallas_call(
        _mlstm_kernel,
        grid=(B, S // L),
        in_specs=[blk(2 * ML_WIDTH), blk(ML_WIDTH), blk(ML_WIDTH), blk(2 * ML_HEADS),
                  pl.BlockSpec((None, 2 * SUBLANES, L), lambda b, c: (b, 0, c)),
                  _const_spec((CONV_WIDTH, 2 * ML_WIDTH)), _const_spec((1, 2 * ML_WIDTH)),
                  _const_spec((L, L)), _const_spec((L, L))],
        out_specs=blk(ML_WIDTH),
        out_shape=jax.ShapeDtypeStruct((B, S, ML_WIDTH), BF16),
        scratch_shapes=[pltpu.VMEM((L + SUBLANES, 2 * ML_WIDTH), F32),
                        pltpu.VMEM((ML_HEADS, ML_HEAD_DIM, 2 * ML_HEAD_DIM), F32),
                        pltpu.VMEM((SUBLANES, LANES), F32)],
        compiler_params=pltpu.CompilerParams(dimension_semantics=("parallel", "arbitrary"),
                                             vmem_limit_bytes=VMEM_LIMIT),
        name="mlstm",
    )(qk_pre, vm, om, gcol, grow, conv_w, conv_b, tril, triu)


def _sb_kernel(q_ref, k_ref, v_ref, tt_ref, out_ref, carry_ref, acc_ref, zd_ref, zn_ref, a_ref, qm_ref):
    T = SB_TILE
    P = SB_TILES_PER_STEP
    assert T == SB_GROUP
    NT = q_ref.shape[0] // T
    assert NT % P == 0
    lane = lax.broadcasted_iota(jnp.int32, (1, LANES), 1)
    first = lane < SB_HEAD_DIM

    def tile(ref, j):
        return ref[pl.ds(pl.multiple_of(j * T, T), T), :]

    def masked_q(j):
        qp = tile(q_ref, j)
        zero = jnp.zeros_like(qp)
        return jnp.where(first, qp, zero), jnp.where(first, zero, qp)

    def weights(t, hh, z, diagonal):
        w = jnp.maximum(z, jnp.log(1.0 + jnp.exp2(jnp.minimum(z, EXP2_MAX))) * LOG2E)
        if diagonal:
            row_i = lax.broadcasted_iota(jnp.int32, (T, T), 0)
            col_i = lax.broadcasted_iota(jnp.int32, (T, T), 1)
            strict = col_i < row_i
            w = jnp.where(strict, w, 0.0)
        arg = _dot(w.astype(BF16), tt_ref[...]) + z
        rowsum = jnp.sum(w, axis=1, keepdims=True)
        if diagonal:
            carry_ref[t, hh] = jnp.broadcast_to(-rowsum, (T, LANES))
            return jnp.where(strict, jnp.exp2(arg), 0.0).astype(BF16)
        carry = carry_ref[t, hh]
        carry_ref[t, hh] = carry - rowsum
        return jnp.exp2(arg + jnp.concatenate([carry] * (T // LANES), axis=1)).astype(BF16)

    def finish(t, j):
        v_diag = tile(v_ref, j)
        v_left = tile(v_ref, jnp.maximum(j - 1, 0))
        out = [acc_ref[t, hh] + _dot(a_ref[t, hh, 0], v_diag) + _dot(a_ref[t, hh, 1], v_left)
               for hh in range(2)]
        res = jnp.where(first, out[0], out[1])
        out_ref[pl.ds(pl.multiple_of(j * T, T), T), :] = res.astype(BF16)
        bits = lax.bitcast_convert_type(res, jnp.uint32)
        return lax.bitcast_convert_type((bits >> 16) >> 16, F32)

    def operands(j):
        j = jnp.minimum(j, NT - 1)
        return masked_q(j), tile(k_ref, j), tile(k_ref, jnp.maximum(j - 1, 0))

    acc_ref[...] = jnp.zeros_like(acc_ref)
    a_ref[...] = jnp.zeros_like(a_ref)
    for t in range(P):
        qn, k_diag, k_left = operands(t)
        for hh in range(2):
            zd_ref[t, hh] = _dot_nt(qn[hh], k_diag)
            zn_ref[t, hh] = _dot_nt(qn[hh], k_left) if t > 0 else jnp.full((T, T), SB_NO_BLOCK, F32)

    def step_body(s, c):
        prev = jnp.maximum(s - 1, 0) * P
        for t in range(P):
            zeros = finish(t, prev + t)
            for hh in range(2):
                acc_ref[t, hh] = zeros
        for t in range(P):
            qm = masked_q(s * P + t)
            for hh in range(2):
                qm_ref[t, hh] = qm[hh]
        nxt = [operands((s + 1) * P + t) for t in range(P)]
        for t in range(P):
            qn, k_diag, _ = nxt[t]
            for hh in range(2):
                a_ref[t, hh, 0] = weights(t, hh, zd_ref[t, hh], True)
                zd_ref[t, hh] = _dot_nt(qn[hh], k_diag)
        for t in range(P):
            qn, _, k_left = nxt[t]
            for hh in range(2):
                a_ref[t, hh, 1] = weights(t, hh, zn_ref[t, hh], False)
                zn_ref[t, hh] = _dot_nt(qn[hh], k_left)

        for t in range(P):
            def live(t=t):
                return jnp.max(carry_ref[t]) >= SB_UNDERFLOW

            def block_cond(state):
                j, more = state
                return jnp.logical_and(j >= 0, more)

            def block_body(state, t=t, live=live):
                j, _ = state
                k_blk = tile(k_ref, j)
                v_blk = tile(v_ref, j)
                for hh in range(2):
                    a = weights(t, hh, _dot_nt(qm_ref[t, hh], k_blk), False)
                    acc_ref[t, hh] += _dot(a, v_blk)
                return j - 1, live()

            lax.while_loop(block_cond, block_body, (s * P + t - 2, live()))
        return c

    lax.fori_loop(0, NT // P, step_body, 0)
    for t in range(P):
        finish(t, NT - P + t)


def _stick_breaking(qs, ks, vs, tt):
    B, S, _ = qs.shape
    T = SB_TILE
    P = SB_TILES_PER_STEP
    seq = pl.BlockSpec((None, S, LANES), lambda b, p: (b, 0, p))
    return pl.pallas_call(
        _sb_kernel,
        grid=(B, SB_WIDTH // LANES),
        in_specs=[seq, seq, seq, _const_spec((SB_GROUP, SB_GROUP))],
        out_specs=seq,
        out_shape=jax.ShapeDtypeStruct((B, S, SB_WIDTH), BF16),
        scratch_shapes=[pltpu.VMEM((P, 2, T, LANES), F32), pltpu.VMEM((P, 2, T, LANES), F32),
                        pltpu.VMEM((P, 2, T, T), F32), pltpu.VMEM((P, 2, T, T), F32),
                        pltpu.VMEM((P, 2, 2, T, T), BF16), pltpu.VMEM((P, 2, T, LANES), BF16)],
        compiler_params=pltpu.CompilerParams(
            dimension_semantics=("parallel", "parallel"),
            vmem_limit_bytes=VMEM_LIMIT),
        name="stickbreak",
    )(qs, ks, vs, tt)


def _mix_ffn_kernel(x_ref, ya_ref, yb_ref, g_ref, wa_ref, wb_ref, wo_ref, gffn_ref, wgu_ref, wd_ref,
                    out_ref):
    pa = _dot(ya_ref[...], wa_ref[...])
    pb = _dot(yb_ref[...], wb_ref[...])
    mix = g_ref[:, :D_MODEL].astype(F32) * pa + g_ref[:, D_MODEL:].astype(F32) * pb
    xn = x_ref[...] + _dot(mix.astype(BF16), wo_ref[...])
    ms = jnp.mean(xn * xn, axis=-1, keepdims=True)
    h2 = (xn * lax.rsqrt(ms + EPS) * gffn_ref[...]).astype(BF16)
    acc = xn
    for c in range(D_FF // FF_CHUNK):
        lo = c * FF_CHUNK
        gt = _dot(h2, wgu_ref[:, lo:lo + FF_CHUNK])
        up = _dot(h2, wgu_ref[:, D_FF + lo:D_FF + lo + FF_CHUNK])
        act = (gt * _sigmoid(gt) * up).astype(BF16)
        acc = acc + _dot(act, wd_ref[lo:lo + FF_CHUNK, :])
    out_ref[...] = acc


def _mix_ffn(x2, layer, ya, yb, g, wa, wb, wo, gffn, wgu, wd):
    T = x2.shape[0]
    tm = ROW_TILE
    row = lambda n: pl.BlockSpec((tm, n), lambda i: (i, 0))
    return pl.pallas_call(
        _mix_ffn_kernel,
        grid=(T // tm,),
        in_specs=[row(D_MODEL), row(ML_WIDTH), row(SB_WIDTH), row(N_BRANCH * D_MODEL),
                  _layer_spec((ML_WIDTH, D_MODEL), layer), _layer_spec((SB_WIDTH, D_MODEL), layer),
                  _layer_spec((D_MODEL, D_MODEL), layer), _const_spec((1, D_MODEL)),
                  _layer_spec((D_MODEL, 2 * D_FF), layer), _layer_spec((D_FF, D_MODEL), layer)],
        out_specs=row(D_MODEL),
        out_shape=jax.ShapeDtypeStruct((T, D_MODEL), F32),
        compiler_params=pltpu.CompilerParams(dimension_semantics=("parallel",),
                                             vmem_limit_bytes=VMEM_LIMIT),
        name="mixffn",
    )(x2, ya, yb, g, wa, wb, wo, gffn, wgu, wd)


def _constants():
    L = ML_CHUNK
    r = np.arange(L)
    tril = (r[None, :] <= r[:, None]).astype(np.float32)
    g = np.arange(SB_WIDTH) // SB_HEAD_DIM
    bd = (g[:, None] == g[None, :]).astype(np.float32)
    s = np.arange(SB_GROUP)
    tt = -(s[:, None] >= s[None, :]).astype(np.float32)
    return (jnp.asarray(tril, BF16), jnp.asarray(tril.T, BF16), jnp.asarray(bd, BF16),
            jnp.asarray(tt, BF16))


def kernel(x, g_mix, w_in, conv_w, conv_b, b_gates, g_q, g_k, w_br_a, w_br_b, w_out, g_ffn, w_gu,
           w_down):
    B, S, D = x.shape
    depth = g_mix.shape[0]
    T = B * S
    tril, triu, bd, tt = _constants()
    n_gate = 2 * ML_HEADS
    o_gates = 2 * ML_WIDTH + 2 * ML_WIDTH

    w_perm = jnp.concatenate(
        [w_in[:, :, :o_gates].astype(BF16), w_in[:, :, o_gates + n_gate:].astype(BF16),
         jnp.pad(w_in[:, :, o_gates:o_gates + n_gate].astype(BF16),
                 ((0, 0), (0, 0), (0, GATE_PAD - n_gate)))],
        axis=2)
    wa, wb, wo = w_br_a.astype(BF16), w_br_b.astype(BF16), w_out.astype(BF16)
    wgu, wd = w_gu.astype(BF16), w_down.astype(BF16)

    x2 = x.reshape(T, D)
    for l in range(depth):
        bg = jnp.pad(b_gates[l], (0, GATE_PAD - n_gate)).reshape(1, GATE_PAD)
        gq = jnp.tile(g_q[l], SB_HEADS).reshape(1, SB_WIDTH)
        gk = jnp.tile(g_k[l], SB_HEADS).reshape(1, SB_WIDTH)

        qk_pre, vm, om, qs, ks, vs, gp, gates = _inproj(
            x2, l, g_mix[l].reshape(1, D), w_perm, bg, gq, gk, bd)

        gcol = gates.reshape(B, S, n_gate)
        grow = jnp.pad(jnp.swapaxes(gcol, 1, 2), ((0, 0), (0, 2 * SUBLANES - n_gate), (0, 0)))
        ya = _mlstm(qk_pre.reshape(B, S, -1), vm.reshape(B, S, -1), om.reshape(B, S, -1),
                    gcol, grow, conv_w[l], conv_b[l].reshape(1, -1), tril, triu)
        yb = _stick_breaking(qs.reshape(B, S, -1), ks.reshape(B, S, -1), vs.reshape(B, S, -1), tt)

        x2 = _mix_ffn(x2, l, ya.reshape(T, -1), yb.reshape(T, -1), gp, wa, wb, wo,
                      g_ffn[l].reshape(1, D), wgu, wd)
    return x2.reshape(B, S, D)
```

```python
import jax
import jax.numpy as jnp
import numpy as np
from jax import lax
from jax.experimental import pallas as pl
from jax.experimental.pallas import tpu as pltpu

D_MODEL = 1024
ML_HEADS = 4
ML_HEAD_DIM = 128
ML_WIDTH = ML_HEADS * ML_HEAD_DIM
CONV_WIDTH = 4
SB_HEADS = 8
SB_HEAD_DIM = 64
SB_WIDTH = SB_HEADS * SB_HEAD_DIM
N_BRANCH = 2
D_FF = 2816
EPS = 1e-6

LANES = 128
SUBLANES = 8
GATE_PAD = LANES
ROW_TILE = 512
ML_CHUNK = 256
SB_TILE = 256
SB_GROUP = 256
SB_TILES_PER_STEP = 1
LOG2E = 1.4426950408889634
EXP2_MAX = 126.0
SB_UNDERFLOW = -160.0
SB_NO_BLOCK = -1.0e4
FF_CHUNK = 256
VMEM_LIMIT = 56 * 1024 * 1024

F32 = jnp.float32
BF16 = jnp.bfloat16

C_QK = 0
C_VM = C_QK + 2 * ML_WIDTH
C_OM = C_VM + ML_WIDTH
C_QS = C_OM + ML_WIDTH
C_KS = C_QS + SB_WIDTH
C_VS = C_KS + SB_WIDTH
C_GP = C_VS + SB_WIDTH
C_GATES = C_GP + N_BRANCH * D_MODEL
C_END = C_GATES + GATE_PAD


def _const_spec(shape):
    nd = len(shape)
    return pl.BlockSpec(shape, lambda *_: (0,) * nd, pipeline_mode=pl.Buffered(1))


def _layer_spec(shape, layer):
    nd = len(shape)
    return pl.BlockSpec((None,) + tuple(shape), lambda *_: (layer,) + (0,) * nd,
                        pipeline_mode=pl.Buffered(1))


def _split3(a):
    hi = a.astype(BF16)
    r = a - hi.astype(F32)
    mid = r.astype(BF16)
    lo = (r - mid.astype(F32)).astype(BF16)
    return hi, mid, lo


def _dot(a, b):
    return jnp.dot(a, b, preferred_element_type=F32)


def _dot_nt(a, b):
    return lax.dot_general(a, b, (((1,), (1,)), ((), ())), preferred_element_type=F32)


def _sigmoid(x):
    return 0.5 * jnp.tanh(0.5 * x) + 0.5


def _inproj_kernel(x_ref, gmix_ref, w_ref, bg_ref, gq_ref, gk_ref, bd_ref,
                   qk_ref, vm_ref, om_ref, qs_ref, ks_ref, vs_ref, gp_ref, gates_ref, grow_ref):
    x = x_ref[...]
    ms = jnp.mean(x * x, axis=-1, keepdims=True)
    h = (x * lax.rsqrt(ms + EPS) * gmix_ref[...]).astype(BF16)

    def proj(a, b):
        return _dot(h, w_ref[:, a:b])

    def head_rms(y, g):
        gs = _dot((y * y).astype(BF16), bd_ref[...])
        return y * lax.rsqrt(gs * (1.0 / SB_HEAD_DIM) + EPS) * g

    qk_ref[...] = proj(C_QK, C_VM)
    vm_ref[...] = proj(C_VM, C_OM).astype(BF16)
    om_ref[...] = _sigmoid(proj(C_OM, C_QS)).astype(BF16)
    qs_ref[...] = (head_rms(proj(C_QS, C_KS), gq_ref[...]) * (SB_HEAD_DIM ** -0.5 * LOG2E)).astype(BF16)
    ks_ref[...] = head_rms(proj(C_KS, C_VS), gk_ref[...]).astype(BF16)
    vs_ref[...] = proj(C_VS, C_GP).astype(BF16)
    gp_ref[...] = _sigmoid(proj(C_GP, C_GATES)).astype(BF16)
    gates = proj(C_GATES, C_END) + bg_ref[...]
    gates_ref[...] = gates[:, :2 * ML_HEADS]
    grow_ref[...] = gates.T[:2 * SUBLANES, :]


def _inproj(x2, layer, gmix, w_perm, bg, gq, gk, bd):
    T = x2.shape[0]
    tm = ROW_TILE
    row = lambda n: pl.BlockSpec((tm, n), lambda i: (i, 0))
    out_shapes = (
        jax.ShapeDtypeStruct((T, 2 * ML_WIDTH), F32),
        jax.ShapeDtypeStruct((T, ML_WIDTH), BF16),
        jax.ShapeDtypeStruct((T, ML_WIDTH), BF16),
        jax.ShapeDtypeStruct((T, SB_WIDTH), BF16),
        jax.ShapeDtypeStruct((T, SB_WIDTH), BF16),
        jax.ShapeDtypeStruct((T, SB_WIDTH), BF16),
        jax.ShapeDtypeStruct((T, N_BRANCH * D_MODEL), BF16),
        jax.ShapeDtypeStruct((T, 2 * ML_HEADS), F32),
        jax.ShapeDtypeStruct((2 * SUBLANES, T), F32),
    )
    return pl.pallas_call(
        _inproj_kernel,
        grid=(T // tm,),
        in_specs=[row(D_MODEL), _const_spec((1, D_MODEL)), _layer_spec((D_MODEL, C_END), layer),
                  _const_spec((1, GATE_PAD)), _const_spec((1, SB_WIDTH)), _const_spec((1, SB_WIDTH)),
                  _const_spec((SB_WIDTH, SB_WIDTH))],
        out_specs=(row(2 * ML_WIDTH), row(ML_WIDTH), row(ML_WIDTH), row(SB_WIDTH), row(SB_WIDTH),
                   row(SB_WIDTH), row(N_BRANCH * D_MODEL), row(2 * ML_HEADS),
                   pl.BlockSpec((2 * SUBLANES, tm), lambda i: (0, i))),
        out_shape=out_shapes,
        compiler_params=pltpu.CompilerParams(dimension_semantics=("parallel",),
                                             vmem_limit_bytes=VMEM_LIMIT),
        name="inproj",
    )(x2, gmix, w_perm, bg, gq, gk, bd)


def _log_sigmoid(x):
    return jnp.minimum(x, 0.0) - jnp.log1p(jnp.exp(-jnp.abs(x)))


def _mlstm_kernel(qk_ref, vm_ref, om_ref, gcol_ref, grow_ref, cw_ref, cb_ref, tril_ref, triu_ref,
                  out_ref, xbuf, c_ref, m_ref):
    L = ML_CHUNK
    H = ML_HEADS
    Dh = ML_HEAD_DIM
    halo = SUBLANES

    @pl.when(pl.program_id(1) == 0)
    def _():
        xbuf[0:halo, :] = jnp.zeros((halo, 2 * ML_WIDTH), F32)
        c_ref[...] = jnp.zeros_like(c_ref)
        m_ref[...] = jnp.zeros_like(m_ref)

    xbuf[halo:halo + L, :] = qk_ref[...]
    acc = jnp.broadcast_to(cb_ref[...], (L, 2 * ML_WIDTH))
    for j in range(CONV_WIDTH):
        off = halo - (CONV_WIDTH - 1) + j
        acc = acc + cw_ref[j:j + 1, :] * xbuf[off:off + L, :]
    xbuf[0:halo, :] = xbuf[L:L + halo, :]
    qkc = acc * _sigmoid(acc)
    q_all = qkc[:, :ML_WIDTH]
    k_all = qkc[:, ML_WIDTH:] * (Dh ** -0.5)

    gc = gcol_ref[...]
    gr = grow_ref[...]
    lfc = _log_sigmoid(gc)
    lfr = _log_sigmoid(gr)
    bcol = sum(_dot(tril_ref[...], p) for p in _split3(lfc))
    brow = sum(_dot(p, triu_ref[...]) for p in _split3(lfr))

    row_i = lax.broadcasted_iota(jnp.int32, (L, L), 0)
    col_i = lax.broadcasted_iota(jnp.int32, (L, L), 1)
    causal = col_i <= row_i
    ones_aug = jnp.ones((L, Dh), BF16)

    for h in range(H):
        bc = bcol[:, H + h:H + h + 1]
        br = brow[H + h:H + h + 1, :]
        ir = gr[h:h + 1, :]
        ic = gc[:, h:h + 1]
        mprev = m_ref[h:h + 1, 0:1]

        logd = jnp.where(causal, bc - br + ir, -jnp.inf)
        mt = jnp.maximum(bc + mprev, jnp.max(logd, axis=-1, keepdims=True))
        w_intra = jnp.exp(logd - mt)
        w_inter = jnp.exp(bc + mprev - mt)

        sl = slice(h * Dh, (h + 1) * Dh)
        qh = q_all[:, sl].astype(BF16)
        kf = k_all[:, sl]
        kh = kf.astype(BF16)
        vaug = jnp.concatenate([vm_ref[:, sl], ones_aug], axis=1)
        caug = c_ref[h]

        s = _dot_nt(qh, kh) * w_intra
        u = w_inter * _dot(qh, caug.astype(BF16)) + _dot(s.astype(BF16), vaug)
        num = u[:, :Dh]
        den = u[:, Dh:]
        hout = num / jnp.maximum(jnp.abs(den), jnp.exp(-mt))
        out_ref[:, sl] = (om_ref[:, sl].astype(F32) * hout).astype(BF16)

        blast = bc[L - 1:L, :]
        logw_c = blast - bc + ic
        logw_r = blast - br + ir
        mnew = jnp.maximum(blast + mprev, jnp.max(logw_r, axis=-1, keepdims=True))
        w_c = jnp.exp(logw_c - mnew)
        decay = jnp.exp(blast + mprev - mnew)
        kwt = (kf * w_c).T.astype(BF16)
        c_ref[h] = decay * caug + _dot(kwt, vaug)
        m_ref[h:h + 1, :] = jnp.broadcast_to(mnew, (1, LANES))


def _mlstm(qk_pre, vm, om, gcol, grow, conv_w, conv_b, tril, triu):
    B, S, _ = qk_pre.shape
    L = ML_CHUNK
    blk = lambda n: pl.BlockSpec((None, L, n), lambda b, c: (b, c, 0))
    return pl.pallas_call(
        _mlstm_kernel,
        grid=(B, S // L),
        in_specs=[blk(2 * ML_WIDTH), blk(ML_WIDTH), blk(ML_WIDTH), blk(2 * ML_HEADS),
                  pl.BlockSpec((2 * SUBLANES, L), lambda b, c: (0, b * (S // L) + c)),
                  _const_spec((CONV_WIDTH, 2 * ML_WIDTH)), _const_spec((1, 2 * ML_WIDTH)),
                  _const_spec((L, L)), _const_spec((L, L))],
        out_specs=blk(ML_WIDTH),
        out_shape=jax.ShapeDtypeStruct((B, S, ML_WIDTH), BF16),
        scratch_shapes=[pltpu.VMEM((L + SUBLANES, 2 * ML_WIDTH), F32),
                        pltpu.VMEM((ML_HEADS, ML_HEAD_DIM, 2 * ML_HEAD_DIM), F32),
                        pltpu.VMEM((SUBLANES, LANES), F32)],
        compiler_params=pltpu.CompilerParams(dimension_semantics=("parallel", "arbitrary"),
                                             vmem_limit_bytes=VMEM_LIMIT),
        name="mlstm",
    )(qk_pre, vm, om, gcol, grow, conv_w, conv_b, tril, triu)


def _sb_kernel(q_ref, k_ref, v_ref, tt_ref, out_ref, carry_ref, acc_ref, zd_ref, zn_ref, a_ref, qm_ref):
    T = SB_TILE
    P = SB_TILES_PER_STEP
    assert T == SB_GROUP
    NT = q_ref.shape[0] // T
    assert NT % P == 0
    lane = lax.broadcasted_iota(jnp.int32, (1, LANES), 1)
    first = lane < SB_HEAD_DIM

    def tile(ref, j):
        return ref[pl.ds(pl.multiple_of(j * T, T), T), :]

    def masked_q(j):
        qp = tile(q_ref, j)
        zero = jnp.zeros_like(qp)
        return jnp.where(first, qp, zero), jnp.where(first, zero, qp)

    def weights(t, hh, z, diagonal):
        w = jnp.maximum(z, jnp.log(1.0 + jnp.exp2(jnp.minimum(z, EXP2_MAX))) * LOG2E)
        if diagonal:
            row_i = lax.broadcasted_iota(jnp.int32, (T, T), 0)
            col_i = lax.broadcasted_iota(jnp.int32, (T, T), 1)
            strict = col_i < row_i
            w = jnp.where(strict, w, 0.0)
        arg = _dot(w.astype(BF16), tt_ref[...]) + z
        rowsum = jnp.sum(w, axis=1, keepdims=True)
        if diagonal:
            carry_ref[t, hh] = jnp.broadcast_to(-rowsum, (T, LANES))
            return jnp.where(strict, jnp.exp2(arg), 0.0).astype(BF16)
        carry = carry_ref[t, hh]
        carry_ref[t, hh] = carry - rowsum
        return jnp.exp2(arg + jnp.concatenate([carry] * (T // LANES), axis=1)).astype(BF16)

    def finish(t, j):
        v_diag = tile(v_ref, j)
        v_left = tile(v_ref, jnp.maximum(j - 1, 0))
        out = [acc_ref[t, hh] + _dot(a_ref[t, hh, 0], v_diag) + _dot(a_ref[t, hh, 1], v_left)
               for hh in range(2)]
        res = jnp.where(first, out[0], out[1])
        out_ref[pl.ds(pl.multiple_of(j * T, T), T), :] = res.astype(BF16)
        bits = lax.bitcast_convert_type(res, jnp.uint32)
        return lax.bitcast_convert_type((bits >> 16) >> 16, F32)

    def operands(j):
        j = jnp.minimum(j, NT - 1)
        return masked_q(j), tile(k_ref, j), tile(k_ref, jnp.maximum(j - 1, 0))

    acc_ref[...] = jnp.zeros_like(acc_ref)
    a_ref[...] = jnp.zeros_like(a_ref)
    for t in range(P):
        qn, k_diag, k_left = operands(t)
        for hh in range(2):
            zd_ref[t, hh] = _dot_nt(qn[hh], k_diag)
            zn_ref[t, hh] = _dot_nt(qn[hh], k_left) if t > 0 else jnp.full((T, T), SB_NO_BLOCK, F32)

    def step_body(s, c):
        prev = jnp.maximum(s - 1, 0) * P
        for t in range(P):
            zeros = finish(t, prev + t)
            for hh in range(2):
                acc_ref[t, hh] = zeros
        for t in range(P):
            qm = masked_q(s * P + t)
            for hh in range(2):
                qm_ref[t, hh] = qm[hh]
        nxt = [operands((s + 1) * P + t) for t in range(P)]
        for t in range(P):
            qn, k_diag, _ = nxt[t]
            for hh in range(2):
                a_ref[t, hh, 0] = weights(t, hh, zd_ref[t, hh], True)
                zd_ref[t, hh] = _dot_nt(qn[hh], k_diag)
        for t in range(P):
            qn, _, k_left = nxt[t]
            for hh in range(2):
                a_ref[t, hh, 1] = weights(t, hh, zn_ref[t, hh], False)
                zn_ref[t, hh] = _dot_nt(qn[hh], k_left)

        for t in range(P):
            def live(t=t):
                return jnp.max(carry_ref[t]) >= SB_UNDERFLOW

            def block_cond(state):
                j, more = state
                return jnp.logical_and(j >= 0, more)

            def block_body(state, t=t, live=live):
                j, _ = state
                k_blk = tile(k_ref, j)
                v_blk = tile(v_ref, j)
                for hh in range(2):
                    a = weights(t, hh, _dot_nt(qm_ref[t, hh], k_blk), False)
                    acc_ref[t, hh] += _dot(a, v_blk)
                return j - 1, live()

            lax.while_loop(block_cond, block_body, (s * P + t - 2, live()))
        return c

    lax.fori_loop(0, NT // P, step_body, 0)
    for t in range(P):
        finish(t, NT - P + t)


def _stick_breaking(qs, ks, vs, tt):
    B, S, _ = qs.shape
    T = SB_TILE
    P = SB_TILES_PER_STEP
    seq = pl.BlockSpec((None, S, LANES), lambda b, p: (b, 0, p))
    return pl.pallas_call(
        _sb_kernel,
        grid=(B, SB_WIDTH // LANES),
        in_specs=[seq, seq, seq, _const_spec((SB_GROUP, SB_GROUP))],
        out_specs=seq,
        out_shape=jax.ShapeDtypeStruct((B, S, SB_WIDTH), BF16),
        scratch_shapes=[pltpu.VMEM((P, 2, T, LANES), F32), pltpu.VMEM((P, 2, T, LANES), F32),
                        pltpu.VMEM((P, 2, T, T), F32), pltpu.VMEM((P, 2, T, T), F32),
                        pltpu.VMEM((P, 2, 2, T, T), BF16), pltpu.VMEM((P, 2, T, LANES), BF16)],
        compiler_params=pltpu.CompilerParams(
            dimension_semantics=("parallel", "parallel"),
            vmem_limit_bytes=VMEM_LIMIT),
        name="stickbreak",
    )(qs, ks, vs, tt)


def _mix_ffn_kernel(x_ref, ya_ref, yb_ref, g_ref, wa_ref, wb_ref, wo_ref, gffn_ref, wgu_ref, wd_ref,
                    out_ref):
    pa = _dot(ya_ref[...], wa_ref[...])
    pb = _dot(yb_ref[...], wb_ref[...])
    mix = g_ref[:, :D_MODEL].astype(F32) * pa + g_ref[:, D_MODEL:].astype(F32) * pb
    xn = x_ref[...] + _dot(mix.astype(BF16), wo_ref[...])
    ms = jnp.mean(xn * xn, axis=-1, keepdims=True)
    h2 = (xn * lax.rsqrt(ms + EPS) * gffn_ref[...]).astype(BF16)
    acc = xn
    for c in range(D_FF // FF_CHUNK):
        lo = c * FF_CHUNK
        gt = _dot(h2, wgu_ref[:, lo:lo + FF_CHUNK])
        up = _dot(h2, wgu_ref[:, D_FF + lo:D_FF + lo + FF_CHUNK])
        act = (gt * _sigmoid(gt) * up).astype(BF16)
        acc = acc + _dot(act, wd_ref[lo:lo + FF_CHUNK, :])
    out_ref[...] = acc


def _mix_ffn(x2, layer, ya, yb, g, wa, wb, wo, gffn, wgu, wd):
    T = x2.shape[0]
    tm = ROW_TILE
    row = lambda n: pl.BlockSpec((tm, n), lambda i: (i, 0))
    return pl.pallas_call(
        _mix_ffn_kernel,
        grid=(T // tm,),
        in_specs=[row(D_MODEL), row(ML_WIDTH), row(SB_WIDTH), row(N_BRANCH * D_MODEL),
                  _layer_spec((ML_WIDTH, D_MODEL), layer), _layer_spec((SB_WIDTH, D_MODEL), layer),
                  _layer_spec((D_MODEL, D_MODEL), layer), _const_spec((1, D_MODEL)),
                  _layer_spec((D_MODEL, 2 * D_FF), layer), _layer_spec((D_FF, D_MODEL), layer)],
        out_specs=row(D_MODEL),
        out_shape=jax.ShapeDtypeStruct((T, D_MODEL), F32),
        compiler_params=pltpu.CompilerParams(dimension_semantics=("parallel",),
                                             vmem_limit_bytes=VMEM_LIMIT),
        name="mixffn",
    )(x2, ya, yb, g, wa, wb, wo, gffn, wgu, wd)


def _constants():
    L = ML_CHUNK
    r = np.arange(L)
    tril = (r[None, :] <= r[:, None]).astype(np.float32)
    g = np.arange(SB_WIDTH) // SB_HEAD_DIM
    bd = (g[:, None] == g[None, :]).astype(np.float32)
    s = np.arange(SB_GROUP)
    tt = -(s[:, None] >= s[None, :]).astype(np.float32)
    return (jnp.asarray(tril, BF16), jnp.asarray(tril.T, BF16), jnp.asarray(bd, BF16),
            jnp.asarray(tt, BF16))


def kernel(x, g_mix, w_in, conv_w, conv_b, b_gates, g_q, g_k, w_br_a, w_br_b, w_out, g_ffn, w_gu,
           w_down):
    B, S, D = x.shape
    depth = g_mix.shape[0]
    T = B * S
    tril, triu, bd, tt = _constants()
    n_gate = 2 * ML_HEADS
    o_gates = 2 * ML_WIDTH + 2 * ML_WIDTH

    w_perm = jnp.concatenate(
        [w_in[:, :, :o_gates].astype(BF16), w_in[:, :, o_gates + n_gate:].astype(BF16),
         jnp.pad(w_in[:, :, o_gates:o_gates + n_gate].astype(BF16),
                 ((0, 0), (0, 0), (0, GATE_PAD - n_gate)))],
        axis=2)
    wa, wb, wo = w_br_a.astype(BF16), w_br_b.astype(BF16), w_out.astype(BF16)
    wgu, wd = w_gu.astype(BF16), w_down.astype(BF16)

    x2 = x.reshape(T, D)
    for l in range(depth):
        bg = jnp.pad(b_gates[l], (0, GATE_PAD - n_gate)).reshape(1, GATE_PAD)
        gq = jnp.tile(g_q[l], SB_HEADS).reshape(1, SB_WIDTH)
        gk = jnp.tile(g_k[l], SB_HEADS).reshape(1, SB_WIDTH)

        qk_pre, vm, om, qs, ks, vs, gp, gates, grow = _inproj(
            x2, l, g_mix[l].reshape(1, D), w_perm, bg, gq, gk, bd)

        gcol = gates.reshape(B, S, n_gate)
        ya = _mlstm(qk_pre.reshape(B, S, -1), vm.reshape(B, S, -1), om.reshape(B, S, -1),
                    gcol, grow, conv_w[l], conv_b[l].reshape(1, -1), tril, triu)
        yb = _stick_breaking(qs.reshape(B, S, -1), ks.reshape(B, S, -1), vs.reshape(B, S, -1), tt)

        x2 = _mix_ffn(x2, l, ya.reshape(T, -1), yb.reshape(T, -1), gp, wa, wb, wo,
                      g_ffn[l].reshape(1, D), wgu, wd)
    return x2.reshape(B, S, D)
```
